```python
import jax, jax.numpy as jnp
from jax import lax
import numpy as np

D_MODEL = 2048
BATCH = 2
SEQ = 8192
DEPTH = 4
DEC_BATCH = 32
DEC_SEQ = 32
PAST_LEN = 2048

CHUNK = 64
D_MIX = D_MODEL
D_CONV = D_MIX // 2
D_ML = D_MIX - D_CONV
CONV_W = 31
ML_HEADS = 4
ML_DH = D_ML // ML_HEADS
N_EXPERTS = 32
N_GROUPS = 4
EXPERTS_PER_GROUP = N_EXPERTS // N_GROUPS
TOP_K = 2
D_FF = D_MODEL // 2
EXPERT_BLOCK = 128
D_IN = 2 * D_CONV + 4 * D_ML + 2 * ML_HEADS
SPLITS = [D_CONV, 2 * D_CONV, 2 * D_CONV + D_ML, 2 * D_CONV + 2 * D_ML,
          2 * D_CONV + 3 * D_ML, 2 * D_CONV + 4 * D_ML, 2 * D_CONV + 4 * D_ML + ML_HEADS]
ALPHA = (2.0 * DEPTH) ** 0.25
BETA = (8.0 * DEPTH) ** -0.25
LN_EPS = 1e-5

kernel_name = 'hymba_conformer_mlstm_grouped_moe_stream_step'


def layernorm(x, g, b):
    xf = x.astype(jnp.float32)
    mu = jnp.mean(xf, axis=-1, keepdims=True)
    var = jnp.mean(jnp.square(xf - mu), axis=-1, keepdims=True)
    y = (xf - mu) * lax.rsqrt(var + LN_EPS) * g.astype(jnp.float32) + b.astype(jnp.float32)
    return y.astype(x.dtype)


def mlstm_chunk(q, k, v, li, lf, C, n, m):
    L = q.shape[2]
    F = jnp.cumsum(lf, axis=-1)
    causal = jnp.tril(jnp.ones((L, L), dtype=bool))
    a = jnp.where(causal, F[..., :, None] - F[..., None, :] + li[..., None, :], -jnp.inf)
    b = F + m[..., None]
    m_t = jnp.maximum(b, jnp.max(a, axis=-1))
    dmat = jnp.exp(a - m_t[..., None])
    inter = jnp.exp(b - m_t)
    s = jnp.einsum('bhtd,bhsd->bhts', q, k) * dmat
    num = jnp.einsum('bhts,bhsv->bhtv', s, v) + inter[..., None] * jnp.einsum('bhtd,bhdv->bhtv', q, C)
    den = jnp.sum(s, axis=-1) + inter * jnp.einsum('bhtd,bhd->bht', q, n)
    h = num / jnp.maximum(jnp.abs(den), jnp.exp(-m_t))[..., None]
    f_tot = F[..., -1]
    g = f_tot[..., None] - F + li
    m_new = jnp.maximum(f_tot + m, jnp.max(g, axis=-1))
    w = jnp.exp(g - m_new[..., None])
    decay = jnp.exp(f_tot + m - m_new)
    C_new = decay[..., None, None] * C + jnp.einsum('bhs,bhsd,bhsv->bhdv', w, k, v)
    n_new = decay[..., None] * n + jnp.einsum('bhs,bhsd->bhd', w, k)
    return h, C_new, n_new, m_new


def mlstm_seq(q, k, v, li, lf, C, n, m):
    B, H, L, dh = q.shape
    if L <= CHUNK:
        return mlstm_chunk(q, k, v, li, lf, C, n, m)
    nc = L // CHUNK

    def blocks(t):
        return jnp.moveaxis(t.reshape((B, H, nc, CHUNK) + t.shape[3:]), 2, 0)

    def step(carry, inp):
        h, C_, n_, m_ = mlstm_chunk(*inp, *carry)
        return (C_, n_, m_), h

    (C, n, m), h = lax.scan(step, (C, n, m), (blocks(q), blocks(k), blocks(v), blocks(li), blocks(lf)))
    h = jnp.moveaxis(h, 0, 2).reshape(B, H, L, dh)
    return h, C, n, m


def token_mixers(x, conv_hist, C0, n0, m0, w_in, b_in, conv_w, conv_b, cln_g, cln_b, mln_g, w_out):
    B, L, _ = x.shape
    z = x @ w_in + b_in
    a, gl, q, k, v, o, ig, fg = jnp.split(z, SPLITS, axis=-1)
    u = a * jax.nn.sigmoid(gl)
    upad = jnp.concatenate([conv_hist.astype(u.dtype), u], axis=1)
    new_hist = upad[:, L:]
    c = lax.conv_general_dilated(upad, conv_w[:, None, :].astype(upad.dtype), (1,), 'VALID',
                                 dimension_numbers=('NWC', 'WIO', 'NWC'),
                                 feature_group_count=D_CONV) + conv_b
    c = jax.nn.silu(layernorm(c, cln_g, cln_b))
    def heads(t):
        return t.reshape(B, L, ML_HEADS, ML_DH).transpose(0, 2, 1, 3).astype(jnp.float32)
    qh = heads(q)
    kh = heads(k) * (ML_DH ** -0.5)
    vh = heads(v)
    li = ig.astype(jnp.float32).transpose(0, 2, 1)
    lf = jax.nn.log_sigmoid(fg.astype(jnp.float32)).transpose(0, 2, 1)
    h, C, n, m = mlstm_seq(qh, kh, vh, li, lf, C0.astype(jnp.float32),
                           n0.astype(jnp.float32), m0.astype(jnp.float32))
    mu = jnp.mean(h, axis=-1, keepdims=True)
    var = jnp.mean(jnp.square(h - mu), axis=-1, keepdims=True)
    hn = ((h - mu) * lax.rsqrt(var + LN_EPS)).transpose(0, 2, 1, 3).reshape(B, L, D_ML)
    hout = (jax.nn.sigmoid(o.astype(jnp.float32)) * hn * mln_g.astype(jnp.float32)).astype(x.dtype)
    out = jnp.concatenate([c, hout], axis=-1) @ w_out
    return out, new_hist, C, n, m


def moe(x2d, router_w, router_bias, wg, wu, wd):
    T, D = x2d.shape
    aff = jax.nn.sigmoid(x2d.astype(jnp.float32) @ router_w.astype(jnp.float32))
    biased = aff + router_bias.astype(jnp.float32)
    grp = biased.reshape(T, N_GROUPS, EXPERTS_PER_GROUP)
    gscore = jnp.sum(lax.top_k(grp, TOP_K)[0], axis=-1)
    gsel = jnp.argmax(gscore, axis=-1)
    in_grp = jnp.take_along_axis(grp, gsel[:, None, None], axis=1)[:, 0]
    _, loc = lax.top_k(in_grp, TOP_K)
    eidx = gsel[:, None] * EXPERTS_PER_GROUP + loc
    ga = jnp.take_along_axis(aff, eidx, axis=1)
    gates = ga / jnp.sum(ga, axis=-1, keepdims=True)
    A = T * TOP_K
    flat_e = eidx.reshape(-1).astype(jnp.int32)
    flat_tok = jnp.repeat(jnp.arange(T, dtype=jnp.int32), TOP_K)
    flat_g = gates.reshape(-1)
    order = jnp.argsort(flat_e)
    se = flat_e[order]
    stok = flat_tok[order]
    sg = flat_g[order]
    counts = jnp.bincount(flat_e, length=N_EXPERTS)
    start = jnp.cumsum(counts) - counts
    pcounts = (counts + EXPERT_BLOCK - 1) // EXPERT_BLOCK * EXPERT_BLOCK
    pend = jnp.cumsum(pcounts)
    pstart = pend - pcounts
    dest = pstart[se] + jnp.arange(A, dtype=jnp.int32) - start[se]
    NB = -(-A // EXPERT_BLOCK) + N_EXPERTS
    P = NB * EXPERT_BLOCK
    tok_buf = jnp.full((P,), T, dtype=jnp.int32).at[dest].set(stok)
    g_buf = jnp.zeros((P,), jnp.float32).at[dest].set(sg)
    blk_e = jnp.minimum(jnp.searchsorted(pend, jnp.arange(NB) * EXPERT_BLOCK, side='right'),
                        N_EXPERTS - 1)
    xpad = jnp.concatenate([x2d, jnp.zeros((1, D), x2d.dtype)], axis=0)
    xb = xpad[tok_buf].reshape(NB, EXPERT_BLOCK, D)

    def expert_block(args):
        xblk, e = args
        hid = jax.nn.silu(xblk @ wg[e]) * (xblk @ wu[e])
        return hid @ wd[e]

    yb = lax.map(expert_block, (xb, blk_e)).reshape(P, D)
    y = jnp.zeros((T + 1, D), jnp.float32).at[tok_buf].add(yb.astype(jnp.float32) * g_buf[:, None])
    return y[:T].astype(x2d.dtype)


def setup_inputs(seed: int = 0) -> dict:
    key = jax.random.key(seed)
    ks = jax.random.split(key, 24)
    nrm = jax.random.normal
    f32 = jnp.float32
    b_in = 0.02 * nrm(ks[7], (DEPTH, D_IN), f32)
    b_in = b_in.at[:, D_IN - ML_HEADS:].add(jnp.linspace(3.0, 6.0, ML_HEADS, dtype=f32))
    return {
        'x_prompt': nrm(ks[0], (BATCH, SEQ, D_MODEL), f32),
        'x_sample': nrm(ks[1], (DEC_BATCH, DEC_SEQ, D_MODEL), f32),
        'state_conv': 0.5 * nrm(ks[2], (DEPTH, DEC_BATCH, CONV_W - 1, D_CONV), f32),
        'state_C': 0.05 * nrm(ks[3], (DEPTH, DEC_BATCH, ML_HEADS, ML_DH, ML_DH), f32),
        'state_n': 0.1 * nrm(ks[4], (DEPTH, DEC_BATCH, ML_HEADS, ML_DH), f32),
        'state_m': 1.0 + 0.5 * nrm(ks[5], (DEPTH, DEC_BATCH, ML_HEADS), f32),
        'w_in': nrm(ks[6], (DEPTH, D_MODEL, D_IN), f32) * D_MODEL ** -0.5,
        'b_in': b_in,
        'conv_w': nrm(ks[8], (DEPTH, CONV_W, D_CONV), f32) * CONV_W ** -0.5,
        'conv_b': 0.02 * nrm(ks[9], (DEPTH, D_CONV), f32),
        'conv_ln_g': 1.0 + 0.02 * nrm(ks[10], (DEPTH, D_CONV), f32),
        'conv_ln_b': 0.02 * nrm(ks[11], (DEPTH, D_CONV), f32),
        'ml_norm_g': 1.0 + 0.02 * nrm(ks[12], (DEPTH, D_ML), f32),
        'w_out': nrm(ks[13], (DEPTH, D_MIX, D_MODEL), f32) * (BETA * D_MIX ** -0.5),
        'ln1_g': 1.0 + 0.02 * nrm(ks[14], (DEPTH, D_MODEL), f32),
        'ln1_b': 0.02 * nrm(ks[15], (DEPTH, D_MODEL), f32),
        'router_w': nrm(ks[16], (D_MODEL, N_EXPERTS), f32) * D_MODEL ** -0.5,
        'router_bias': 0.01 * nrm(ks[17], (N_EXPERTS,), f32),
        'w_gate': nrm(ks[18], (DEPTH, N_EXPERTS, D_MODEL, D_FF), f32) * D_MODEL ** -0.5,
        'w_up': nrm(ks[19], (DEPTH, N_EXPERTS, D_MODEL, D_FF), f32) * D_MODEL ** -0.5,
        'w_down': nrm(ks[20], (DEPTH, N_EXPERTS, D_FF, D_MODEL), f32) * (BETA * D_FF ** -0.5),
        'ln2_g': 1.0 + 0.02 * nrm(ks[21], (DEPTH, D_MODEL), f32),
        'ln2_b': 0.02 * nrm(ks[22], (DEPTH, D_MODEL), f32),
    }


def reference(x_prompt, x_sample, state_conv, state_C, state_n, state_m, w_in, b_in, conv_w,
              conv_b, conv_ln_g, conv_ln_b, ml_norm_g, w_out, ln1_g, ln1_b, router_w, router_bias,
              w_gate, w_up, w_down, ln2_g, ln2_b):
    def layer(l, x, hist, C, n, m):
        mix, hist, C, n, m = token_mixers(x, hist, C, n, m, w_in[l], b_in[l], conv_w[l], conv_b[l],
                                          conv_ln_g[l], conv_ln_b[l], ml_norm_g[l], w_out[l])
        x = layernorm(ALPHA * x + mix, ln1_g[l], ln1_b[l])
        B, L, _ = x.shape
        ff = moe(x.reshape(B * L, D_MODEL), router_w, router_bias,
                 w_gate[l], w_up[l], w_down[l]).reshape(B, L, D_MODEL)
        x = layernorm(ALPHA * x + ff, ln2_g[l], ln2_b[l])
        return x, hist, C, n, m

    yp = x_prompt
    pc, pC, pn, pm = [], [], [], []
    for l in range(DEPTH):
        hist0 = jnp.zeros((BATCH, CONV_W - 1, D_CONV), x_prompt.dtype)
        C0 = jnp.zeros((BATCH, ML_HEADS, ML_DH, ML_DH), jnp.float32)
        n0 = jnp.zeros((BATCH, ML_HEADS, ML_DH), jnp.float32)
        m0 = jnp.zeros((BATCH, ML_HEADS), jnp.float32)
        yp, h_, C_, n_, m_ = layer(l, yp, hist0, C0, n0, m0)
        pc.append(h_); pC.append(C_); pn.append(n_); pm.append(m_)

    ys = x_sample
    sc, sC, sn, sm = [], [], [], []
    for l in range(DEPTH):
        ys, h_, C_, n_, m_ = layer(l, ys, state_conv[l], state_C[l], state_n[l], state_m[l])
        sc.append(h_); sC.append(C_); sn.append(n_); sm.append(m_)

    return (yp, ys, jnp.stack(pc), jnp.stack(pC), jnp.stack(pn), jnp.stack(pm),
            jnp.stack(sc), jnp.stack(sC), jnp.stack(sn), jnp.stack(sm))
```

```python
import functools

import jax
import jax.numpy as jnp
from jax import lax
from jax.experimental import pallas as pl
from jax.experimental.pallas import tpu as pltpu

F32 = jnp.float32
BF16 = jnp.bfloat16

N_GROUPS = 4
TOP_K = 2
LN_EPS = 1e-5
LANES = 128
SUBLANES = 8
VMEM_LIMIT = 56 * 1024 * 1024


def _tile(n, target, align=SUBLANES):
    t = min(n, target)
    t -= t % align
    while t >= align:
        if n % t == 0:
            return t
        t -= align
    return n


def _params(*sem):
    return pltpu.CompilerParams(dimension_semantics=sem, vmem_limit_bytes=VMEM_LIMIT)


def _ln_rows(y, g, b):
    mu = jnp.mean(y, axis=-1, keepdims=True)
    d = y - mu
    var = jnp.mean(d * d, axis=-1, keepdims=True)
    return d * lax.rsqrt(var + LN_EPS) * g + b


def _log_sigmoid(x):
    return jnp.minimum(x, 0.0) - jnp.log1p(jnp.exp(-jnp.abs(x)))


def _glu_kernel(x_ref, wa_ref, wg_ref, ba_ref, bg_ref, wgt_ref, bgt_ref, u_ref, gates_ref):
    x = x_ref[...]
    a = jnp.dot(x, wa_ref[...], preferred_element_type=F32) + ba_ref[...]
    g = jnp.dot(x, wg_ref[...], preferred_element_type=F32) + bg_ref[...]
    u_ref[...] = a * jax.nn.sigmoid(g)

    @pl.when(pl.program_id(1) == 0)
    def _():
        gates_ref[...] = jnp.dot(x, wgt_ref[...], preferred_element_type=F32) + bgt_ref[...]


def _glu_proj(xb, wa, wg, ba, bg, wgt, bgt):
    T, D = xb.shape
    DC = wa.shape[1]
    TM = _tile(T, 1024)
    TN = _tile(DC, 512, LANES)
    return pl.pallas_call(
        _glu_kernel,
        grid=(T // TM, DC // TN),
        in_specs=[
            pl.BlockSpec((TM, D), lambda m, n: (m, 0)),
            pl.BlockSpec((D, TN), lambda m, n: (0, n)),
            pl.BlockSpec((D, TN), lambda m, n: (0, n)),
            pl.BlockSpec((1, TN), lambda m, n: (0, n)),
            pl.BlockSpec((1, TN), lambda m, n: (0, n)),
            pl.BlockSpec((D, LANES), lambda m, n: (0, 0)),
            pl.BlockSpec((1, LANES), lambda m, n: (0, 0)),
        ],
        out_specs=[
            pl.BlockSpec((TM, TN), lambda m, n: (m, n)),
            pl.BlockSpec((TM, LANES), lambda m, n: (m, 0)),
        ],
        out_shape=[jax.ShapeDtypeStruct((T, DC), F32), jax.ShapeDtypeStruct((T, LANES), F32)],
        compiler_params=_params("parallel", "arbitrary"),
        name="glu_proj",
    )(xb, wa, wg, ba, bg, wgt, bgt)


def _mm_kernel(x_ref, w_ref, b_ref, o_ref):
    acc = jnp.dot(x_ref[...], w_ref[...], preferred_element_type=F32) + b_ref[...]
    o_ref[...] = acc.astype(o_ref.dtype)


def _mm_bias(xb, w, b, out_dtype):
    T, D = xb.shape
    N = w.shape[1]
    TM = _tile(T, 1024)
    TN = _tile(N, 1024, LANES)
    return pl.pallas_call(
        _mm_kernel,
        grid=(T // TM, N // TN),
        in_specs=[
            pl.BlockSpec((TM, D), lambda m, n: (m, 0)),
            pl.BlockSpec((D, TN), lambda m, n: (0, n)),
            pl.BlockSpec((1, TN), lambda m, n: (0, n)),
        ],
        out_specs=pl.BlockSpec((TM, TN), lambda m, n: (m, n)),
        out_shape=jax.ShapeDtypeStruct((T, N), out_dtype),
        compiler_params=_params("parallel", "parallel"),
        name="qkvo_proj",
    )(xb, w, b)


def _conv_kernel(u_ref, hist_ref, w_ref, cb_ref, g_ref, b_ref, c_ref, nh_ref, ext, cbuf,
                 *, TT, KW, HP, RC, CC):
    t = pl.program_id(1)
    H = KW - 1
    DC = u_ref.shape[-1]

    @pl.when(t == 0)
    def _():
        ext[HP - H:HP, :] = hist_ref[...]

    ext[HP:HP + TT, :] = u_ref[...]

    for r0 in range(0, TT, RC):
        for c0 in range(0, DC, CC):
            acc = None
            for b in range(SUBLANES):
                na = (KW - 1 - b) // SUBLANES + 1
                win = ext[HP - H + r0 + b:HP - H + r0 + b + RC + SUBLANES * (na - 1), c0:c0 + CC]
                for a in range(na):
                    j = SUBLANES * a + b
                    term = w_ref[j:j + 1, c0:c0 + CC] * win[SUBLANES * a:SUBLANES * a + RC]
                    acc = term if acc is None else acc + term
            cbuf[r0:r0 + RC, c0:c0 + CC] = acc + cb_ref[:, c0:c0 + CC]

    y = _ln_rows(cbuf[...], g_ref[...], b_ref[...])
    c_ref[...] = (y * jax.nn.sigmoid(y)).astype(c_ref.dtype)

    tail = ext[HP + TT - H:HP + TT, :]
    ext[HP - H:HP, :] = tail

    @pl.when(t == pl.num_programs(1) - 1)
    def _():
        nh_ref[...] = tail


def _conv_branch(u, hist, conv_w, conv_b, g, b):
    B, L, DC = u.shape
    KW = conv_w.shape[0]
    TT = _tile(L, 512)
    assert TT >= KW - 1 and TT % SUBLANES == 0
    HP = -(-(KW - 1) // SUBLANES) * SUBLANES
    RC = _tile(TT, 128)
    CC = LANES
    kern = functools.partial(_conv_kernel, TT=TT, KW=KW, HP=HP, RC=RC, CC=CC)
    return pl.pallas_call(
        kern,
        grid=(B, L // TT),
        in_specs=[
            pl.BlockSpec((None, TT, DC), lambda bi, t: (bi, t, 0)),
            pl.BlockSpec((None, KW - 1, DC), lambda bi, t: (bi, 0, 0)),
            pl.BlockSpec((KW, DC), lambda bi, t: (0, 0)),
            pl.BlockSpec((1, DC), lambda bi, t: (0, 0)),
            pl.BlockSpec((1, DC), lambda bi, t: (0, 0)),
            pl.BlockSpec((1, DC), lambda bi, t: (0, 0)),
        ],
        out_specs=[
            pl.BlockSpec((None, TT, DC), lambda bi, t: (bi, t, 0)),
            pl.BlockSpec((None, KW - 1, DC), lambda bi, t: (bi, 0, 0)),
        ],
        out_shape=[jax.ShapeDtypeStruct((B, L, DC), BF16), jax.ShapeDtypeStruct((B, KW - 1, DC), F32)],
        scratch_shapes=[pltpu.VMEM((HP + TT, DC), F32), pltpu.VMEM((TT, DC), F32)],
        compiler_params=_params("parallel", "arbitrary"),
        name="conv_branch",
    )(u, hist, conv_w, conv_b, g, b)


def _mlstm_kernel(q_ref, k_ref, v_ref, o_ref, gc_ref, gr_ref, c0_ref, n0_ref, m0_ref, mg_ref,
                  h_ref, cN_ref, nN_ref, mN_ref, c_scr, n_scr, m_scr, *, NH, LC):
    hd = pl.program_id(1)
    ck = pl.program_id(2)
    dh = q_ref.shape[-1]
    scale = dh ** -0.5

    @pl.when(ck == 0)
    def _():
        c_scr[...] = c0_ref[...]
        n_scr[...] = n0_ref[...]
        m_scr[...] = m0_ref[...]

    gcol = gc_ref[...]
    lane = lax.broadcasted_iota(jnp.int32, gcol.shape, 1)
    li_col = jnp.sum(jnp.where(lane == hd, gcol, 0.0), axis=1, keepdims=True)
    lf_col = _log_sigmoid(jnp.sum(jnp.where(lane == NH + hd, gcol, 0.0), axis=1, keepdims=True))
    grow = gr_ref[...]
    sub = lax.broadcasted_iota(jnp.int32, grow.shape, 0)
    li_row = jnp.sum(jnp.where(sub == hd, grow, 0.0), axis=0, keepdims=True)
    lf_row = _log_sigmoid(jnp.sum(jnp.where(sub == NH + hd, grow, 0.0), axis=0, keepdims=True))

    ti = lax.broadcasted_iota(jnp.int32, (LC, LC), 0)
    si = lax.broadcasted_iota(jnp.int32, (LC, LC), 1)
    causal = si <= ti
    f_col = jnp.sum(jnp.where(causal, lf_row, 0.0), axis=1, keepdims=True)
    f_row = jnp.sum(jnp.where(ti <= si, lf_col, 0.0), axis=0, keepdims=True)
    g_row = li_row - f_row
    g_col = li_col - f_col

    m_prev = m_scr[:, 0:1]
    a = jnp.where(causal, g_row, -jnp.inf)
    m_col = jnp.maximum(m_prev, jnp.max(a, axis=1, keepdims=True))
    dmat = jnp.exp(a - m_col)
    inter = jnp.exp(m_prev - m_col)

    q = q_ref[...]
    k = k_ref[...]
    v = v_ref[...]
    c_prev = c_scr[...]
    n_prev = n_scr[...]
    s = lax.dot_general(q, k, (((1,), (1,)), ((), ())), preferred_element_type=F32) * scale * dmat
    num = (jnp.dot(s.astype(BF16), v, preferred_element_type=F32)
           + inter * jnp.dot(q, c_prev.astype(BF16), preferred_element_type=F32))
    den = (jnp.sum(s, axis=1, keepdims=True)
           + inter * jnp.sum(q.astype(F32) * n_prev, axis=1, keepdims=True))
    hcore = num / jnp.maximum(jnp.abs(den), jnp.exp(-(f_col + m_col)))

    mu = jnp.mean(hcore, axis=1, keepdims=True)
    d = hcore - mu
    var = jnp.mean(d * d, axis=1, keepdims=True)
    hn = d * lax.rsqrt(var + LN_EPS)
    h_ref[...] = (jax.nn.sigmoid(o_ref[...].astype(F32)) * hn * mg_ref[...]).astype(h_ref.dtype)

    m_last = m_col[LC - 1:LC, :]
    f_tot = f_col[LC - 1:LC, :]
    decay = jnp.exp(m_prev - m_last)
    kw = k.astype(F32) * (jnp.exp(g_col - m_last) * scale)
    c_new = decay * c_prev + lax.dot_general(kw.astype(BF16), v, (((0,), (0,)), ((), ())),
                                             preferred_element_type=F32)
    n_new = decay * n_prev + jnp.sum(kw, axis=0, keepdims=True)
    m_new = jnp.broadcast_to(f_tot + m_last, m_scr.shape)
    c_scr[...] = c_new
    n_scr[...] = n_new
    m_scr[...] = m_new
    cN_ref[...] = c_new
    nN_ref[...] = n_new
    mN_ref[...] = m_new


def _mlstm_branch(qkvo, gates, gates_t, c0, n0, m0, mln_g, B, L):
    NH, dh = c0.shape[1], c0.shape[2]
    DML = NH * dh
    LC = _tile(L, 256, LANES) if L % LANES == 0 else L
    NC = L // LC
    kern = functools.partial(_mlstm_kernel, NH=NH, LC=LC)
    row = lambda b, h, c: b * NC + c
    qkv_spec = lambda off: pl.BlockSpec((LC, dh), lambda b, h, c: (row(b, h, c), off * NH + h))
    state4 = lambda r, w: pl.BlockSpec((None, None, r, w), lambda b, h, c: (b, h, 0, 0))
    return pl.pallas_call(
        kern,
        grid=(B, NH, NC),
        in_specs=[
            qkv_spec(0), qkv_spec(1), qkv_spec(2), qkv_spec(3),
            pl.BlockSpec((LC, LANES), lambda b, h, c: (row(b, h, c), 0)),
            pl.BlockSpec((None, 2 * NH, LC), lambda b, h, c: (b, 0, c)),
            state4(dh, dh), state4(1, dh), state4(1, LANES),
            pl.BlockSpec((1, dh), lambda b, h, c: (0, h)),
        ],
        out_specs=[
            pl.BlockSpec((LC, dh), lambda b, h, c: (row(b, h, c), h)),
            state4(dh, dh), state4(1, dh), state4(1, LANES),
        ],
        out_shape=[
            jax.ShapeDtypeStruct((B * L, DML), BF16),
            jax.ShapeDtypeStruct((B, NH, dh, dh), F32),
            jax.ShapeDtypeStruct((B, NH, 1, dh), F32),
            jax.ShapeDtypeStruct((B, NH, 1, LANES), F32),
        ],
        scratch_shapes=[pltpu.VMEM((dh, dh), F32), pltpu.VMEM((1, dh), F32), pltpu.VMEM((1, LANES), F32)],
        compiler_params=_params("parallel", "parallel", "arbitrary"),
        name="mlstm_branch",
    )(qkvo, qkvo, qkvo, qkvo, gates, gates_t, c0, n0, m0, mln_g)


def _outproj_kernel(c_ref, h_ref, x_ref, wc_ref, wh_ref, g_ref, b_ref, rw_ref, rb_ref, cnt0_ref,
                    x1_ref, xp_ref, info_ref, cnt_ref, run_scr, *, alpha, NE):
    i = pl.program_id(0)
    TM, D = x_ref.shape

    @pl.when(i == 0)
    def _():
        run_scr[...] = cnt0_ref[...]

    mix = (jnp.dot(c_ref[...], wc_ref[...], preferred_element_type=F32)
           + jnp.dot(h_ref[...], wh_ref[...], preferred_element_type=F32))
    x1 = _ln_rows(alpha * x_ref[...] + mix, g_ref[...], b_ref[...])
    x1_ref[...] = x1

    half = D // 2
    lo = lax.bitcast_convert_type(x1[:, :half].astype(BF16).astype(F32), jnp.uint32)
    hi = lax.bitcast_convert_type(x1[:, half:].astype(BF16).astype(F32), jnp.uint32)
    xp_ref[...] = (hi & jnp.uint32(0xFFFF0000)) | (lo >> 16)

    logits = jnp.dot(x1, rw_ref[...], preferred_element_type=F32, precision=lax.Precision.HIGHEST)
    aff = jax.nn.sigmoid(logits)
    lane = lax.broadcasted_iota(jnp.int32, (TM, LANES), 1)
    lanef = lane.astype(F32)
    neg = -jnp.inf
    biased = jnp.where(lane < NE, aff + rb_ref[...], neg)
    epg = NE // N_GROUPS
    best = e0 = e1 = None
    for grp in range(N_GROUPS):
        vg = jnp.where((lane >= grp * epg) & (lane < (grp + 1) * epg), biased, neg)
        m1 = jnp.max(vg, axis=1, keepdims=True)
        i1 = jnp.min(jnp.where(vg == m1, lanef, float(LANES)), axis=1, keepdims=True)
        vg2 = jnp.where(lanef == i1, neg, vg)
        m2 = jnp.max(vg2, axis=1, keepdims=True)
        i2 = jnp.min(jnp.where(vg2 == m2, lanef, float(LANES)), axis=1, keepdims=True)
        score = m1 + m2
        if grp == 0:
            best, e0, e1 = score, i1, i2
        else:
            better = score > best
            best = jnp.where(better, score, best)
            e0 = jnp.where(better, i1, e0)
            e1 = jnp.where(better, i2, e1)

    oh0 = lanef == e0
    oh1 = lanef == e1
    a0 = jnp.sum(jnp.where(oh0, aff, 0.0), axis=1, keepdims=True)
    a1 = jnp.sum(jnp.where(oh1, aff, 0.0), axis=1, keepdims=True)
    asum = a0 + a1
    onehot = jnp.where(oh0 | oh1, 1.0, 0.0)
    ri = lax.broadcasted_iota(jnp.int32, (TM, TM), 0)
    ci = lax.broadcasted_iota(jnp.int32, (TM, TM), 1)
    tri = jnp.where(ci < ri, 1.0, 0.0).astype(BF16)
    prefix = jnp.dot(tri, onehot.astype(BF16), preferred_element_type=F32) + run_scr[...]
    r0 = jnp.sum(jnp.where(oh0, prefix, 0.0), axis=1, keepdims=True)
    r1 = jnp.sum(jnp.where(oh1, prefix, 0.0), axis=1, keepdims=True)
    run = run_scr[...] + jnp.sum(onehot, axis=0, keepdims=True)
    run_scr[...] = run
    cnt_ref[...] = run

    cols = (e0, e1, r0, r1, a0 / asum, a1 / asum)
    info = jnp.zeros((TM, LANES), F32)
    for ci_, val in enumerate(cols):
        info = jnp.where(lane == ci_, val, info)
    info_ref[...] = info


def _outproj_router(c, h, x, wc, wh, g, b, rw, rb, cnt0, alpha, NE):
    T, D = x.shape
    DC, DML = c.shape[1], h.shape[1]
    TM = _tile(T, 512)
    kern = functools.partial(_outproj_kernel, alpha=alpha, NE=NE)
    rowspec = lambda w: pl.BlockSpec((TM, w), lambda i: (i, 0))
    full = lambda r, w: pl.BlockSpec((r, w), lambda i: (0, 0))
    return pl.pallas_call(
        kern,
        grid=(T // TM,),
        in_specs=[rowspec(DC), rowspec(DML), rowspec(D), full(DC, D), full(DML, D), full(1, D), full(1, D),
                  full(D, LANES), full(1, LANES), full(1, LANES)],
        out_specs=[rowspec(D), rowspec(D // 2), rowspec(LANES), full(1, LANES)],
        out_shape=[
            jax.ShapeDtypeStruct((T, D), F32),
            jax.ShapeDtypeStruct((T, D // 2), jnp.uint32),
            jax.ShapeDtypeStruct((T, LANES), F32),
            jax.ShapeDtypeStruct((1, LANES), F32),
        ],
        scratch_shapes=[pltpu.VMEM((1, LANES), F32)],
        compiler_params=_params("arbitrary"),
        name="outproj_router",
    )(c, h, x, wc, wh, g, b, rw, rb, cnt0)


def _scatter_kernel(dest_ref, x_hbm, xb_in, xb_out, sem, *, TMS):
    del xb_in
    base = pl.program_id(0) * TMS

    def copy(src_row, dst_row):
        return pltpu.make_async_copy(x_hbm.at[pl.ds(src_row, 1)], xb_out.at[pl.ds(dst_row, 1)], sem)

    def issue(r, carry):
        for kk in range(TOP_K):
            copy(base + r, dest_ref[TOP_K * r + kk]).start()
        return carry

    lax.fori_loop(0, TMS, issue, 0)

    def drain(r, carry):
        for kk in range(TOP_K):
            copy(base + r, dest_ref[TOP_K * r + kk]).wait()
        return carry

    lax.fori_loop(0, TMS, drain, 0)


def _scatter_rows(dest_flat, xp, xb):
    T = xp.shape[0]
    TMS = _tile(T, 512)
    kern = functools.partial(_scatter_kernel, TMS=TMS)
    return pl.pallas_call(
        kern,
        grid=(T // TMS,),
        in_specs=[
            pl.BlockSpec((TOP_K * TMS,), lambda i: (i,), memory_space=pltpu.SMEM),
            pl.BlockSpec(memory_space=pl.ANY),
            pl.BlockSpec(memory_space=pl.ANY),
        ],
        out_specs=pl.BlockSpec(memory_space=pl.ANY),
        out_shape=jax.ShapeDtypeStruct(xb.shape, xb.dtype),
        scratch_shapes=[pltpu.SemaphoreType.DMA(())],
        input_output_aliases={2: 0},
        compiler_params=_params("arbitrary"),
        name="moe_scatter",
    )(dest_flat, xp, xb)


def _expert_kernel(blk_e, blk_x, blk_n, xb_ref, wg_ref, wu_ref, wd_ref, o_ref, xs, wgs, wus, wds, *, SUB):
    del blk_e, blk_x
    i = pl.program_id(0)
    j = pl.program_id(1)
    nsub = blk_n[i]
    half = xb_ref.shape[-1]

    @pl.when(nsub > 0)
    def _():
        @pl.when(j == 0)
        def _():
            def unpack(sidx, carry):
                rows = pl.ds(pl.multiple_of(sidx * SUB, SUB), SUB)
                w = xb_ref[rows, :]
                lo = lax.bitcast_convert_type(w << 16, F32)
                hi = lax.bitcast_convert_type(w & jnp.uint32(0xFFFF0000), F32)
                xs[rows, 0:half] = lo.astype(BF16)
                xs[rows, half:2 * half] = hi.astype(BF16)
                return carry

            lax.fori_loop(0, nsub, unpack, 0)

        wgs[...] = wg_ref[...].astype(BF16)
        wus[...] = wu_ref[...].astype(BF16)
        wds[...] = wd_ref[...].astype(BF16)

        def body(sidx, carry):
            rows = pl.ds(pl.multiple_of(sidx * SUB, SUB), SUB)
            x = xs[rows, :]
            g = jnp.dot(x, wgs[...], preferred_element_type=F32)
            u = jnp.dot(x, wus[...], preferred_element_type=F32)
            hid = (g * jax.nn.sigmoid(g) * u).astype(BF16)
            y = jnp.dot(hid, wds[...], preferred_element_type=F32)

            @pl.when(j == 0)
            def _():
                o_ref[rows, :] = y

            @pl.when(j > 0)
            def _():
                o_ref[rows, :] += y

            return carry

        lax.fori_loop(0, nsub, body, 0)


def _expert_ffn(layer, blk_e, blk_x, blk_n, xb, w_gate, w_up, w_down, SB, SUB):
    P, half = xb.shape
    D = 2 * half
    DFF = w_gate.shape[-1]
    TF = _tile(DFF, 256, LANES)
    NF = DFF // TF
    NB = P // SB
    kern = functools.partial(_expert_kernel, SUB=SUB)

    def fidx(i, j, bn):
        return jnp.where(bn[i] > 0, j, NF - 1)

    grid_spec = pltpu.PrefetchScalarGridSpec(
        num_scalar_prefetch=3,
        grid=(NB, NF),
        in_specs=[
            pl.BlockSpec((SB, half), lambda i, j, be, bx, bn: (bx[i], 0)),
            pl.BlockSpec((None, None, D, TF), lambda i, j, be, bx, bn: (layer, be[i], 0, fidx(i, j, bn))),
            pl.BlockSpec((None, None, D, TF), lambda i, j, be, bx, bn: (layer, be[i], 0, fidx(i, j, bn))),
            pl.BlockSpec((None, None, TF, D), lambda i, j, be, bx, bn: (layer, be[i], fidx(i, j, bn), 0)),
        ],
        out_specs=pl.BlockSpec((SB, D), lambda i, j, be, bx, bn: (bx[i], 0)),
        scratch_shapes=[pltpu.VMEM((SB, D), BF16), pltpu.VMEM((D, TF), BF16), pltpu.VMEM((D, TF), BF16),
                        pltpu.VMEM((TF, D), BF16)],
    )
    return pl.pallas_call(
        kern,
        grid_spec=grid_spec,
        out_shape=jax.ShapeDtypeStruct((P, D), F32),
        compiler_params=_params("arbitrary", "arbitrary"),
        name="expert_ffn",
    )(blk_e, blk_x, blk_n, xb, w_gate, w_up, w_down)


def _combine_kernel(dest_ref, x1_ref, info_ref, g_ref, b_ref, yb_hbm, x2_ref, x2b_ref, buf, sem,
                    *, alpha, TMC):
    def copy(r, kk):
        return pltpu.make_async_copy(yb_hbm.at[pl.ds(dest_ref[TOP_K * r + kk], 1)],
                                     buf.at[kk, pl.ds(r, 1)], sem)

    def issue(r, carry):
        for kk in range(TOP_K):
            copy(r, kk).start()
        return carry

    lax.fori_loop(0, TMC, issue, 0)

    def drain(r, carry):
        for kk in range(TOP_K):
            copy(r, kk).wait()
        return carry

    lax.fori_loop(0, TMC, drain, 0)

    info = info_ref[...]
    y = info[:, 4:5] * buf[0] + info[:, 5:6] * buf[1]
    x2 = _ln_rows(alpha * x1_ref[...] + y, g_ref[...], b_ref[...])
    x2_ref[...] = x2
    x2b_ref[...] = x2.astype(BF16)


def _combine_ln(dest_flat, x1, info, g, b, yb, alpha):
    T, D = x1.shape
    TMC = _tile(T, 512)
    kern = functools.partial(_combine_kernel, alpha=alpha, TMC=TMC)
    rowspec = lambda w: pl.BlockSpec((TMC, w), lambda i: (i, 0))
    return pl.pallas_call(
        kern,
        grid=(T // TMC,),
        in_specs=[
            pl.BlockSpec((TOP_K * TMC,), lambda i: (i,), memory_space=pltpu.SMEM),
            rowspec(D), rowspec(LANES),
            pl.BlockSpec((1, D), lambda i: (0, 0)), pl.BlockSpec((1, D), lambda i: (0, 0)),
            pl.BlockSpec(memory_space=pl.ANY),
        ],
        out_specs=[rowspec(D), rowspec(D)],
        out_shape=[jax.ShapeDtypeStruct((T, D), F32), jax.ShapeDtypeStruct((T, D), BF16)],
        scratch_shapes=[pltpu.VMEM((TOP_K, TMC, D), F32), pltpu.SemaphoreType.DMA(())],
        compiler_params=_params("arbitrary"),
        name="moe_combine",
    )(dest_flat, x1, info, g, b, yb)


def _dispatch_plan(cnt, infos, NE, SB, SUB, NB):
    counts = cnt[0, :NE].astype(jnp.int32)
    pcounts = (counts + SB - 1) // SB * SB
    pend = jnp.cumsum(pcounts)
    pstart = pend - pcounts
    dests = []
    for info in infos:
        e = info[:, 0:TOP_K].astype(jnp.int32)
        rank = info[:, TOP_K:2 * TOP_K].astype(jnp.int32)
        onehot = (e[..., None] == jnp.arange(NE, dtype=jnp.int32)).astype(jnp.int32)
        dests.append((jnp.sum(onehot * pstart, axis=-1) + rank).reshape(-1))
    blk0 = jnp.arange(NB, dtype=jnp.int32) * SB
    n_used = jnp.maximum(pend[-1] // SB, 1)
    blk_x = jnp.minimum(jnp.arange(NB, dtype=jnp.int32), n_used - 1)
    blk_e = jnp.minimum(jnp.searchsorted(pend, blk_x * SB, side='right'), NE - 1).astype(jnp.int32)
    valid = jnp.clip(pstart[blk_e] + counts[blk_e] - blk0, 0, SB)
    blk_n = ((valid + SUB - 1) // SUB).astype(jnp.int32)
    return dests, blk_e, blk_x.astype(jnp.int32), blk_n


def kernel(x_prompt, x_sample, state_conv, state_C, state_n, state_m, w_in, b_in, conv_w, conv_b,
           conv_ln_g, conv_ln_b, ml_norm_g, w_out, ln1_g, ln1_b, router_w, router_bias,
           w_gate, w_up, w_down, ln2_g, ln2_b):
    depth, D, _ = w_in.shape
    DC = conv_w.shape[-1]
    NH, dh = state_C.shape[2], state_C.shape[3]
    DML = NH * dh
    NE = router_w.shape[1]
    alpha = (2.0 * depth) ** 0.25
    KW = conv_w.shape[1]

    streams = []
    Bp, Lp, _ = x_prompt.shape
    Bs, Ls, _ = x_sample.shape
    streams.append(dict(B=Bp, L=Lp, x=x_prompt.reshape(Bp * Lp, D), prompt=True))
    streams.append(dict(B=Bs, L=Ls, x=x_sample.reshape(Bs * Ls, D), prompt=False))
    for s in streams:
        s['xb'] = s['x'].astype(BF16)
        s['outs'] = dict(conv=[], C=[], n=[], m=[])

    A = TOP_K * (Bp * Lp + Bs * Ls)
    SB = 1024 if A >= 8192 else 128
    SUB = SB // 4
    NB = -(-A // SB) + NE
    P = NB * SB

    rw = jnp.zeros((D, LANES), F32).at[:, :NE].set(router_w.astype(F32))
    rb = jnp.zeros((1, LANES), F32).at[0, :NE].set(router_bias.astype(F32))
    row2 = lambda v: v.reshape(1, -1)

    for l in range(depth):
        wl = w_in[l]
        wa = wl[:, :DC].astype(BF16)
        wg = wl[:, DC:2 * DC].astype(BF16)
        wqkvo = wl[:, 2 * DC:2 * DC + 4 * DML].astype(BF16)
        wgt = jnp.zeros((D, LANES), BF16).at[:, :2 * NH].set(wl[:, 2 * DC + 4 * DML:].astype(BF16))
        bl = b_in[l]
        bgt = jnp.zeros((1, LANES), F32).at[0, :2 * NH].set(bl[2 * DC + 4 * DML:])
        wc = w_out[l, :DC].astype(BF16)
        wh = w_out[l, DC:].astype(BF16)

        cnt = jnp.zeros((1, LANES), F32)
        for s in streams:
            B, L = s['B'], s['L']
            u, gates = _glu_proj(s['xb'], wa, wg, row2(bl[:DC]), row2(bl[DC:2 * DC]), wgt, bgt)
            qkvo = _mm_bias(s['xb'], wqkvo, row2(bl[2 * DC:2 * DC + 4 * DML]), BF16)
            if s['prompt']:
                hist = jnp.zeros((B, KW - 1, DC), F32)
                c0 = jnp.zeros((B, NH, dh, dh), F32)
                n0 = jnp.zeros((B, NH, 1, dh), F32)
                m0 = jnp.zeros((B, NH, 1, LANES), F32)
            else:
                hist = state_conv[l]
                c0 = state_C[l]
                n0 = state_n[l].reshape(B, NH, 1, dh)
                m0 = jnp.broadcast_to(state_m[l].reshape(B, NH, 1, 1), (B, NH, 1, LANES))
            c, new_hist = _conv_branch(u.reshape(B, L, DC), hist, conv_w[l], row2(conv_b[l]),
                                       row2(conv_ln_g[l]), row2(conv_ln_b[l]))
            gates_t = gates[:, :2 * NH].reshape(B, L, 2 * NH).transpose(0, 2, 1)
            hout, cN, nN, mN = _mlstm_branch(qkvo, gates, gates_t, c0, n0, m0, row2(ml_norm_g[l]), B, L)
            s['outs']['conv'].append(new_hist)
            s['outs']['C'].append(cN)
            s['outs']['n'].append(nN.reshape(B, NH, dh))
            s['outs']['m'].append(mN[:, :, 0, 0])
            x1, xp, info, cnt = _outproj_router(c.reshape(B * L, DC), hout, s['x'], wc, wh,
                                                row2(ln1_g[l]), row2(ln1_b[l]), rw, rb, cnt, alpha, NE)
            s['x1'], s['xp'], s['info'] = x1, xp, info

        dests, blk_e, blk_x, blk_n = _dispatch_plan(cnt, [s['info'] for s in streams], NE, SB, SUB, NB)
        xb = jnp.zeros((P, D // 2), jnp.uint32)
        for s, dest in zip(streams, dests):
            xb = _scatter_rows(dest, s['xp'], xb)
        yb = _expert_ffn(l, blk_e, blk_x, blk_n, xb, w_gate, w_up, w_down, SB, SUB)
        for s, dest in zip(streams, dests):
            s['x'], s['xb'] = _combine_ln(dest, s['x1'], s['info'], row2(ln2_g[l]), row2(ln2_b[l]), yb, alpha)

    outs = []
    for s in streams:
        outs.append(s['x'].reshape(s['B'], s['L'], D))
    for s in streams:
        o = s['outs']
        outs += [jnp.stack(o['conv']), jnp.stack(o['C']), jnp.stack(o['n']), jnp.stack(o['m'])]
    return tuple(outs)
```

```python
import functools

import jax
import jax.numpy as jnp
from jax import lax
from jax.experimental import pallas as pl
from jax.experimental.pallas import tpu as pltpu

F32 = jnp.float32
BF16 = jnp.bfloat16

N_GROUPS = 4
TOP_K = 2
LN_EPS = 1e-5
LANES = 128
SUBLANES = 8
VMEM_LIMIT = 56 * 1024 * 1024
DMA_UNROLL = 8
ROW_BLOCK = 256


def _tile(n, target, align=SUBLANES):
    t = min(n, target)
    t -= t % align
    while t >= align:
        if n % t == 0:
            return t
        t -= align
    return n


def _params(*sem):
    return pltpu.CompilerParams(dimension_semantics=sem, vmem_limit_bytes=VMEM_LIMIT)


def _ln_rows(y, g, b):
    mu = jnp.mean(y, axis=-1, keepdims=True)
    d = y - mu
    var = jnp.mean(d * d, axis=-1, keepdims=True)
    return d * lax.rsqrt(var + LN_EPS) * g + b


def _log_sigmoid(x):
    return jnp.minimum(x, 0.0) - jnp.log1p(jnp.exp(-jnp.abs(x)))


def _glu_kernel(x_ref, wa_ref, wg_ref, ba_ref, bg_ref, wgt_ref, bgt_ref, u_ref, gates_ref):
    x = x_ref[...]
    a = jnp.dot(x, wa_ref[...], preferred_element_type=F32) + ba_ref[...]
    g = jnp.dot(x, wg_ref[...], preferred_element_type=F32) + bg_ref[...]
    u_ref[...] = a * jax.nn.sigmoid(g)

    @pl.when(pl.program_id(1) == 0)
    def _():
        gates_ref[...] = jnp.dot(x, wgt_ref[...], preferred_element_type=F32) + bgt_ref[...]


def _glu_proj(xb, wa, wg, ba, bg, wgt, bgt):
    T, D = xb.shape
    DC = wa.shape[1]
    TM = _tile(T, 1024)
    TN = _tile(DC, 512, LANES)
    return pl.pallas_call(
        _glu_kernel,
        grid=(T // TM, DC // TN),
        in_specs=[
            pl.BlockSpec((TM, D), lambda m, n: (m, 0)),
            pl.BlockSpec((D, TN), lambda m, n: (0, n)),
            pl.BlockSpec((D, TN), lambda m, n: (0, n)),
            pl.BlockSpec((1, TN), lambda m, n: (0, n)),
            pl.BlockSpec((1, TN), lambda m, n: (0, n)),
            pl.BlockSpec((D, LANES), lambda m, n: (0, 0)),
            pl.BlockSpec((1, LANES), lambda m, n: (0, 0)),
        ],
        out_specs=[
            pl.BlockSpec((TM, TN), lambda m, n: (m, n)),
            pl.BlockSpec((TM, LANES), lambda m, n: (m, 0)),
        ],
        out_shape=[jax.ShapeDtypeStruct((T, DC), F32), jax.ShapeDtypeStruct((T, LANES), F32)],
        compiler_params=_params("parallel", "arbitrary"),
        name="glu_proj",
    )(xb, wa, wg, ba, bg, wgt, bgt)


def _mm_kernel(x_ref, w_ref, b_ref, o_ref):
    acc = jnp.dot(x_ref[...], w_ref[...], preferred_element_type=F32) + b_ref[...]
    o_ref[...] = acc.astype(o_ref.dtype)


def _mm_bias(xb, w, b, out_dtype):
    T, D = xb.shape
    N = w.shape[1]
    TM = _tile(T, 1024)
    TN = _tile(N, 1024, LANES)
    return pl.pallas_call(
        _mm_kernel,
        grid=(T // TM, N // TN),
        in_specs=[
            pl.BlockSpec((TM, D), lambda m, n: (m, 0)),
            pl.BlockSpec((D, TN), lambda m, n: (0, n)),
            pl.BlockSpec((1, TN), lambda m, n: (0, n)),
        ],
        out_specs=pl.BlockSpec((TM, TN), lambda m, n: (m, n)),
        out_shape=jax.ShapeDtypeStruct((T, N), out_dtype),
        compiler_params=_params("parallel", "parallel"),
        name="qkvo_proj",
    )(xb, w, b)


def _conv_kernel(u_ref, hist_ref, w_ref, cb_ref, g_ref, b_ref, c_ref, nh_ref, ext, cbuf,
                 *, TT, KW, HP, RC, CC):
    t = pl.program_id(1)
    H = KW - 1
    DC = u_ref.shape[-1]

    @pl.when(t == 0)
    def _():
        ext[HP - H:HP, :] = hist_ref[...]

    ext[HP:HP + TT, :] = u_ref[...]

    for r0 in range(0, TT, RC):
        for c0 in range(0, DC, CC):
            acc = None
            for b in range(SUBLANES):
                na = (KW - 1 - b) // SUBLANES + 1
                win = ext[HP - H + r0 + b:HP - H + r0 + b + RC + SUBLANES * (na - 1), c0:c0 + CC]
                for a in range(na):
                    j = SUBLANES * a + b
                    term = w_ref[j:j + 1, c0:c0 + CC] * win[SUBLANES * a:SUBLANES * a + RC]
                    acc = term if acc is None else acc + term
            cbuf[r0:r0 + RC, c0:c0 + CC] = acc + cb_ref[:, c0:c0 + CC]

    y = _ln_rows(cbuf[...], g_ref[...], b_ref[...])
    c_ref[...] = (y * jax.nn.sigmoid(y)).astype(c_ref.dtype)

    tail = ext[HP + TT - H:HP + TT, :]
    ext[HP - H:HP, :] = tail

    @pl.when(t == pl.num_programs(1) - 1)
    def _():
        nh_ref[...] = tail


def _conv_branch(u, hist, conv_w, conv_b, g, b):
    B, L, DC = u.shape
    KW = conv_w.shape[0]
    TT = _tile(L, 512)
    assert TT >= KW - 1 and TT % SUBLANES == 0
    HP = -(-(KW - 1) // SUBLANES) * SUBLANES
    RC = _tile(TT, 128)
    CC = LANES
    kern = functools.partial(_conv_kernel, TT=TT, KW=KW, HP=HP, RC=RC, CC=CC)
    return pl.pallas_call(
        kern,
        grid=(B, L // TT),
        in_specs=[
            pl.BlockSpec((None, TT, DC), lambda bi, t: (bi, t, 0)),
            pl.BlockSpec((None, KW - 1, DC), lambda bi, t: (bi, 0, 0)),
            pl.BlockSpec((KW, DC), lambda bi, t: (0, 0)),
            pl.BlockSpec((1, DC), lambda bi, t: (0, 0)),
            pl.BlockSpec((1, DC), lambda bi, t: (0, 0)),
            pl.BlockSpec((1, DC), lambda bi, t: (0, 0)),
        ],
        out_specs=[
            pl.BlockSpec((None, TT, DC), lambda bi, t: (bi, t, 0)),
            pl.BlockSpec((None, KW - 1, DC), lambda bi, t: (bi, 0, 0)),
        ],
        out_shape=[jax.ShapeDtypeStruct((B, L, DC), BF16), jax.ShapeDtypeStruct((B, KW - 1, DC), F32)],
        scratch_shapes=[pltpu.VMEM((HP + TT, DC), F32), pltpu.VMEM((TT, DC), F32)],
        compiler_params=_params("parallel", "arbitrary"),
        name="conv_branch",
    )(u, hist, conv_w, conv_b, g, b)


def _mlstm_kernel(q_ref, k_ref, v_ref, o_ref, gc_ref, gr_ref, c0_ref, n0_ref, m0_ref, mg_ref,
                  h_ref, cN_ref, nN_ref, mN_ref, c_scr, n_scr, m_scr, *, NH, LC):
    hd = pl.program_id(1)
    ck = pl.program_id(2)
    dh = q_ref.shape[-1]
    scale = dh ** -0.5

    @pl.when(ck == 0)
    def _():
        c_scr[...] = c0_ref[...]
        n_scr[...] = n0_ref[...]
        m_scr[...] = m0_ref[...]

    gcol = gc_ref[...]
    lane = lax.broadcasted_iota(jnp.int32, gcol.shape, 1)
    li_col = jnp.sum(jnp.where(lane == hd, gcol, 0.0), axis=1, keepdims=True)
    lf_col = _log_sigmoid(jnp.sum(jnp.where(lane == NH + hd, gcol, 0.0), axis=1, keepdims=True))
    grow = gr_ref[...]
    sub = lax.broadcasted_iota(jnp.int32, grow.shape, 0)
    li_row = jnp.sum(jnp.where(sub == hd, grow, 0.0), axis=0, keepdims=True)
    lf_row = _log_sigmoid(jnp.sum(jnp.where(sub == NH + hd, grow, 0.0), axis=0, keepdims=True))

    ti = lax.broadcasted_iota(jnp.int32, (LC, LC), 0)
    si = lax.broadcasted_iota(jnp.int32, (LC, LC), 1)
    causal = si <= ti
    f_col = jnp.sum(jnp.where(causal, lf_row, 0.0), axis=1, keepdims=True)
    f_row = jnp.sum(jnp.where(ti <= si, lf_col, 0.0), axis=0, keepdims=True)
    g_row = li_row - f_row
    g_col = li_col - f_col

    m_prev = m_scr[:, 0:1]
    a = jnp.where(causal, g_row, -jnp.inf)
    m_col = jnp.maximum(m_prev, jnp.max(a, axis=1, keepdims=True))
    dmat = jnp.exp(a - m_col)
    inter = jnp.exp(m_prev - m_col)

    q = q_ref[...]
    k = k_ref[...]
    v = v_ref[...]
    c_prev = c_scr[...]
    n_prev = n_scr[...]
    s = lax.dot_general(q, k, (((1,), (1,)), ((), ())), preferred_element_type=F32) * scale * dmat
    num = (jnp.dot(s.astype(BF16), v, preferred_element_type=F32)
           + inter * jnp.dot(q, c_prev.astype(BF16), preferred_element_type=F32))
    den = (jnp.sum(s, axis=1, keepdims=True)
           + inter * jnp.sum(q.astype(F32) * n_prev, axis=1, keepdims=True))
    hcore = num / jnp.maximum(jnp.abs(den), jnp.exp(-(f_col + m_col)))

    mu = jnp.mean(hcore, axis=1, keepdims=True)
    d = hcore - mu
    var = jnp.mean(d * d, axis=1, keepdims=True)
    hn = d * lax.rsqrt(var + LN_EPS)
    h_ref[...] = (jax.nn.sigmoid(o_ref[...].astype(F32)) * hn * mg_ref[...]).astype(h_ref.dtype)

    m_last = m_col[LC - 1:LC, :]
    f_tot = f_col[LC - 1:LC, :]
    decay = jnp.exp(m_prev - m_last)
    kw = k.astype(F32) * (jnp.exp(g_col - m_last) * scale)
    c_new = decay * c_prev + lax.dot_general(kw.astype(BF16), v, (((0,), (0,)), ((), ())),
                                             preferred_element_type=F32)
    n_new = decay * n_prev + jnp.sum(kw, axis=0, keepdims=True)
    m_new = jnp.broadcast_to(f_tot + m_last, m_scr.shape)
    c_scr[...] = c_new
    n_scr[...] = n_new
    m_scr[...] = m_new
    cN_ref[...] = c_new
    nN_ref[...] = n_new
    mN_ref[...] = m_new


def _mlstm_branch(qkvo, gates, gates_t, c0, n0, m0, mln_g, B, L):
    NH, dh = c0.shape[1], c0.shape[2]
    DML = NH * dh
    LC = _tile(L, 256, LANES) if L % LANES == 0 else L
    NC = L // LC
    kern = functools.partial(_mlstm_kernel, NH=NH, LC=LC)
    row = lambda b, h, c: b * NC + c
    qkv_spec = lambda off: pl.BlockSpec((LC, dh), lambda b, h, c: (row(b, h, c), off * NH + h))
    state4 = lambda r, w: pl.BlockSpec((None, None, r, w), lambda b, h, c: (b, h, 0, 0))
    return pl.pallas_call(
        kern,
        grid=(B, NH, NC),
        in_specs=[
            qkv_spec(0), qkv_spec(1), qkv_spec(2), qkv_spec(3),
            pl.BlockSpec((LC, LANES), lambda b, h, c: (row(b, h, c), 0)),
            pl.BlockSpec((None, 2 * NH, LC), lambda b, h, c: (b, 0, c)),
            state4(dh, dh), state4(1, dh), state4(1, LANES),
            pl.BlockSpec((1, dh), lambda b, h, c: (0, h)),
        ],
        out_specs=[
            pl.BlockSpec((LC, dh), lambda b, h, c: (row(b, h, c), h)),
            state4(dh, dh), state4(1, dh), state4(1, LANES),
        ],
        out_shape=[
            jax.ShapeDtypeStruct((B * L, DML), BF16),
            jax.ShapeDtypeStruct((B, NH, dh, dh), F32),
            jax.ShapeDtypeStruct((B, NH, 1, dh), F32),
            jax.ShapeDtypeStruct((B, NH, 1, LANES), F32),
        ],
        scratch_shapes=[pltpu.VMEM((dh, dh), F32), pltpu.VMEM((1, dh), F32), pltpu.VMEM((1, LANES), F32)],
        compiler_params=_params("parallel", "parallel", "arbitrary"),
        name="mlstm_branch",
    )(qkvo, qkvo, qkvo, qkvo, gates, gates_t, c0, n0, m0, mln_g)


def _outproj_kernel(c_ref, h_ref, x_ref, wc_ref, wh_ref, g_ref, b_ref, rw_ref, rb_ref, cnt0_ref,
                    x1_ref, xp_ref, info_ref, cnt_ref, run_scr, *, alpha, NE):
    i = pl.program_id(0)
    TM, D = x_ref.shape

    @pl.when(i == 0)
    def _():
        run_scr[...] = cnt0_ref[...]

    mix = (jnp.dot(c_ref[...], wc_ref[...], preferred_element_type=F32)
           + jnp.dot(h_ref[...], wh_ref[...], preferred_element_type=F32))
    x1 = _ln_rows(alpha * x_ref[...] + mix, g_ref[...], b_ref[...])
    x1_ref[...] = x1

    half = D // 2
    x1h = x1.astype(BF16)
    x1hf = x1h.astype(F32)
    lo = lax.bitcast_convert_type(x1hf[:, :half], jnp.uint32)
    hi = lax.bitcast_convert_type(x1hf[:, half:], jnp.uint32)
    xp_ref[...] = (hi & jnp.uint32(0xFFFF0000)) | (lo >> 16)

    x1l = (x1 - x1hf).astype(BF16)
    p = jnp.dot(x1h, rw_ref[...], preferred_element_type=F32)
    logits = (p[:, :LANES] + p[:, LANES:]
              + jnp.dot(x1l, rw_ref[:, :LANES], preferred_element_type=F32))
    aff = jax.nn.sigmoid(logits)
    lane = lax.broadcasted_iota(jnp.int32, (TM, LANES), 1)
    lanef = lane.astype(F32)
    neg = -jnp.inf
    biased = jnp.where(lane < NE, aff + rb_ref[...], neg)
    epg = NE // N_GROUPS
    best = e0 = e1 = None
    for grp in range(N_GROUPS):
        vg = jnp.where((lane >= grp * epg) & (lane < (grp + 1) * epg), biased, neg)
        m1 = jnp.max(vg, axis=1, keepdims=True)
        i1 = jnp.min(jnp.where(vg == m1, lanef, float(LANES)), axis=1, keepdims=True)
        vg2 = jnp.where(lanef == i1, neg, vg)
        m2 = jnp.max(vg2, axis=1, keepdims=True)
        i2 = jnp.min(jnp.where(vg2 == m2, lanef, float(LANES)), axis=1, keepdims=True)
        score = m1 + m2
        if grp == 0:
            best, e0, e1 = score, i1, i2
        else:
            better = score > best
            best = jnp.where(better, score, best)
            e0 = jnp.where(better, i1, e0)
            e1 = jnp.where(better, i2, e1)

    oh0 = lanef == e0
    oh1 = lanef == e1
    a0 = jnp.sum(jnp.where(oh0, aff, 0.0), axis=1, keepdims=True)
    a1 = jnp.sum(jnp.where(oh1, aff, 0.0), axis=1, keepdims=True)
    asum = a0 + a1
    onehot = jnp.where(oh0 | oh1, 1.0, 0.0)
    ri = lax.broadcasted_iota(jnp.int32, (TM, TM), 0)
    ci = lax.broadcasted_iota(jnp.int32, (TM, TM), 1)
    tri = jnp.where(ci < ri, 1.0, 0.0).astype(BF16)
    prefix = jnp.dot(tri, onehot.astype(BF16), preferred_element_type=F32) + run_scr[...]
    r0 = jnp.sum(jnp.where(oh0, prefix, 0.0), axis=1, keepdims=True)
    r1 = jnp.sum(jnp.where(oh1, prefix, 0.0), axis=1, keepdims=True)
    run = run_scr[...] + jnp.sum(onehot, axis=0, keepdims=True)
    run_scr[...] = run
    cnt_ref[...] = run

    cols = (e0, e1, r0, r1, a0 / asum, a1 / asum)
    info = jnp.zeros((TM, LANES), F32)
    for ci_, val in enumerate(cols):
        info = jnp.where(lane == ci_, val, info)
    info_ref[...] = info


def _outproj_router(c, h, x, wc, wh, g, b, rw, rb, cnt0, alpha, NE):
    T, D = x.shape
    DC, DML = c.shape[1], h.shape[1]
    TM = _tile(T, 512)
    kern = functools.partial(_outproj_kernel, alpha=alpha, NE=NE)
    rowspec = lambda w: pl.BlockSpec((TM, w), lambda i: (i, 0))
    full = lambda r, w: pl.BlockSpec((r, w), lambda i: (0, 0))
    return pl.pallas_call(
        kern,
        grid=(T // TM,),
        in_specs=[rowspec(DC), rowspec(DML), rowspec(D), full(DC, D), full(DML, D), full(1, D), full(1, D),
                  full(D, 2 * LANES), full(1, LANES), full(1, LANES)],
        out_specs=[rowspec(D), rowspec(D // 2), rowspec(LANES), full(1, LANES)],
        out_shape=[
            jax.ShapeDtypeStruct((T, D), F32),
            jax.ShapeDtypeStruct((T, D // 2), jnp.uint32),
            jax.ShapeDtypeStruct((T, LANES), F32),
            jax.ShapeDtypeStruct((1, LANES), F32),
        ],
        scratch_shapes=[pltpu.VMEM((1, LANES), F32)],
        compiler_params=_params("arbitrary"),
        name="outproj_router",
    )(c, h, x, wc, wh, g, b, rw, rb, cnt0)


def _scatter_kernel(dest_ref, x_ref, xb_in, xb_out, sem, *, TMS):
    del xb_in

    def copy(r, kk):
        return pltpu.make_async_copy(x_ref.at[pl.ds(r, 1)],
                                     xb_out.at[pl.ds(dest_ref[TOP_K * r + kk], 1)], sem)

    def issue(r, carry):
        for kk in range(TOP_K):
            copy(r, kk).start()
        return carry

    lax.fori_loop(0, TMS, issue, 0, unroll=DMA_UNROLL)

    def drain(r, carry):
        for kk in range(TOP_K):
            copy(r, kk).wait()
        return carry

    lax.fori_loop(0, TMS, drain, 0, unroll=DMA_UNROLL)


def _scatter_rows(dest_flat, xp, xb):
    T, half = xp.shape
    TMS = _tile(T, 512)
    kern = functools.partial(_scatter_kernel, TMS=TMS)
    return pl.pallas_call(
        kern,
        grid=(T // TMS,),
        in_specs=[
            pl.BlockSpec((TOP_K * TMS,), lambda i: (i,), memory_space=pltpu.SMEM),
            pl.BlockSpec((TMS, half), lambda i: (i, 0)),
            pl.BlockSpec(memory_space=pl.ANY),
        ],
        out_specs=pl.BlockSpec(memory_space=pl.ANY),
        out_shape=jax.ShapeDtypeStruct(xb.shape, xb.dtype),
        scratch_shapes=[pltpu.SemaphoreType.DMA(())],
        input_output_aliases={2: 0},
        compiler_params=_params("arbitrary"),
        name="moe_scatter",
    )(dest_flat, xp, xb)


def _new_expert(i, blk_e):
    return (i == 0) | (blk_e[i] != blk_e[jnp.maximum(i - 1, 0)])


def _expert_up_kernel(blk_e, blk_x, n_used, xb_ref, wg_ref, wu_ref, hid_ref, wgs, wus):
    del blk_x
    i = pl.program_id(1)
    half = xb_ref.shape[-1]

    @pl.when(i < n_used[0])
    def _():
        @pl.when(_new_expert(i, blk_e))
        def _():
            wgs[...] = wg_ref[...].astype(BF16)
            wus[...] = wu_ref[...].astype(BF16)

        w = xb_ref[...]
        lo = lax.bitcast_convert_type(w << 16, F32).astype(BF16)
        hi = lax.bitcast_convert_type(w & jnp.uint32(0xFFFF0000), F32).astype(BF16)
        g = (jnp.dot(lo, wgs[0:half, :], preferred_element_type=F32)
             + jnp.dot(hi, wgs[half:2 * half, :], preferred_element_type=F32))
        u = (jnp.dot(lo, wus[0:half, :], preferred_element_type=F32)
             + jnp.dot(hi, wus[half:2 * half, :], preferred_element_type=F32))
        hid_ref[...] = (g * jax.nn.sigmoid(g) * u).astype(hid_ref.dtype)


def _expert_down_kernel(blk_e, blk_x, n_used, hid_ref, wd_ref, y_ref, wds):
    del blk_x
    i = pl.program_id(1)

    @pl.when(i < n_used[0])
    def _():
        @pl.when(_new_expert(i, blk_e))
        def _():
            wds[...] = wd_ref[...].astype(BF16)

        y_ref[...] = jnp.dot(hid_ref[...], wds[...], preferred_element_type=F32)


def _expert_ffn(layer, blk_e, blk_x, n_used, xb, w_gate, w_up, w_down, RB):
    P, half = xb.shape
    D = 2 * half
    DFF = w_gate.shape[-1]
    NBLK = P // RB
    TF = _tile(DFF, 1024, LANES)
    TN = _tile(D, 2048, LANES)
    wspec = lambda r, c: pl.BlockSpec((None, None, r, c), lambda n, i, be, bx, nu: (layer, be[i], 0, n))
    rows_full = lambda c: pl.BlockSpec((RB, c), lambda n, i, be, bx, nu: (bx[i], 0))
    rows_tile = lambda c: pl.BlockSpec((RB, c), lambda n, i, be, bx, nu: (bx[i], n))
    hid = pl.pallas_call(
        _expert_up_kernel,
        grid_spec=pltpu.PrefetchScalarGridSpec(
            num_scalar_prefetch=3,
            grid=(DFF // TF, NBLK),
            in_specs=[rows_full(half), wspec(D, TF), wspec(D, TF)],
            out_specs=rows_tile(TF),
            scratch_shapes=[pltpu.VMEM((D, TF), BF16), pltpu.VMEM((D, TF), BF16)],
        ),
        out_shape=jax.ShapeDtypeStruct((P, DFF), BF16),
        compiler_params=_params("arbitrary", "arbitrary"),
        name="expert_up",
    )(blk_e, blk_x, n_used, xb, w_gate, w_up)
    return pl.pallas_call(
        _expert_down_kernel,
        grid_spec=pltpu.PrefetchScalarGridSpec(
            num_scalar_prefetch=3,
            grid=(D // TN, NBLK),
            in_specs=[rows_full(DFF), wspec(DFF, TN)],
            out_specs=rows_tile(TN),
            scratch_shapes=[pltpu.VMEM((DFF, TN), BF16)],
        ),
        out_shape=jax.ShapeDtypeStruct((P, D), F32),
        compiler_params=_params("arbitrary", "arbitrary"),
        name="expert_down",
    )(blk_e, blk_x, n_used, hid, w_down)


def _combine_kernel(dest_ref, x1_ref, info_ref, g_ref, b_ref, yb_hbm, x2_ref, x2b_ref, buf, sem,
                    *, alpha, TMC):
    def copy(r, kk):
        return pltpu.make_async_copy(yb_hbm.at[pl.ds(dest_ref[TOP_K * r + kk], 1)],
                                     buf.at[kk, pl.ds(r, 1)], sem)

    def issue(r, carry):
        for kk in range(TOP_K):
            copy(r, kk).start()
        return carry

    lax.fori_loop(0, TMC, issue, 0, unroll=DMA_UNROLL)

    def drain(r, carry):
        for kk in range(TOP_K):
            copy(r, kk).wait()
        return carry

    lax.fori_loop(0, TMC, drain, 0, unroll=DMA_UNROLL)

    info = info_ref[...]
    y = info[:, 4:5] * buf[0] + info[:, 5:6] * buf[1]
    x2 = _ln_rows(alpha * x1_ref[...] + y, g_ref[...], b_ref[...])
    x2_ref[...] = x2
    x2b_ref[...] = x2.astype(BF16)


def _combine_ln(dest_flat, x1, info, g, b, yb, alpha):
    T, D = x1.shape
    TMC = _tile(T, 512)
    kern = functools.partial(_combine_kernel, alpha=alpha, TMC=TMC)
    rowspec = lambda w: pl.BlockSpec((TMC, w), lambda i: (i, 0))
    return pl.pallas_call(
        kern,
        grid=(T // TMC,),
        in_specs=[
            pl.BlockSpec((TOP_K * TMC,), lambda i: (i,), memory_space=pltpu.SMEM),
            rowspec(D), rowspec(LANES),
            pl.BlockSpec((1, D), lambda i: (0, 0)), pl.BlockSpec((1, D), lambda i: (0, 0)),
            pl.BlockSpec(memory_space=pl.ANY),
        ],
        out_specs=[rowspec(D), rowspec(D)],
        out_shape=[jax.ShapeDtypeStruct((T, D), F32), jax.ShapeDtypeStruct((T, D), BF16)],
        scratch_shapes=[pltpu.VMEM((TOP_K, TMC, D), F32), pltpu.SemaphoreType.DMA(())],
        compiler_params=_params("arbitrary"),
        name="moe_combine",
    )(dest_flat, x1, info, g, b, yb)


def _dispatch_plan(cnt, infos, NE, RB, NBLK):
    counts = cnt[0, :NE].astype(jnp.int32)
    pcounts = (counts + RB - 1) // RB * RB
    pend = jnp.cumsum(pcounts)
    pstart = pend - pcounts
    dests = []
    for info in infos:
        e = info[:, 0:TOP_K].astype(jnp.int32)
        rank = info[:, TOP_K:2 * TOP_K].astype(jnp.int32)
        onehot = (e[..., None] == jnp.arange(NE, dtype=jnp.int32)).astype(jnp.int32)
        dests.append((jnp.sum(onehot * pstart, axis=-1) + rank).reshape(-1))
    n_used = jnp.maximum(pend[-1] // RB, 1).astype(jnp.int32)
    blk_x = jnp.minimum(jnp.arange(NBLK, dtype=jnp.int32), n_used - 1)
    blk_e = jnp.minimum(jnp.searchsorted(pend, blk_x * RB, side='right'), NE - 1).astype(jnp.int32)
    return dests, blk_e, blk_x, n_used.reshape(1)


def kernel(x_prompt, x_sample, state_conv, state_C, state_n, state_m, w_in, b_in, conv_w, conv_b,
           conv_ln_g, conv_ln_b, ml_norm_g, w_out, ln1_g, ln1_b, router_w, router_bias,
           w_gate, w_up, w_down, ln2_g, ln2_b):
    depth, D, _ = w_in.shape
    DC = conv_w.shape[-1]
    NH, dh = state_C.shape[2], state_C.shape[3]
    DML = NH * dh
    NE = router_w.shape[1]
    alpha = (2.0 * depth) ** 0.25
    KW = conv_w.shape[1]

    streams = []
    Bp, Lp, _ = x_prompt.shape
    Bs, Ls, _ = x_sample.shape
    streams.append(dict(B=Bp, L=Lp, x=x_prompt.reshape(Bp * Lp, D), prompt=True))
    streams.append(dict(B=Bs, L=Ls, x=x_sample.reshape(Bs * Ls, D), prompt=False))
    for s in streams:
        s['xb'] = s['x'].astype(BF16)
        s['outs'] = dict(conv=[], C=[], n=[], m=[])

    A = TOP_K * (Bp * Lp + Bs * Ls)
    RB = ROW_BLOCK if A >= 32 * ROW_BLOCK else 32
    NBLK = -(-A // RB) + NE
    P = NBLK * RB

    rw_f = jnp.zeros((D, LANES), F32).at[:, :NE].set(router_w.astype(F32))
    rw_hi = rw_f.astype(BF16)
    rw = jnp.concatenate([rw_hi, (rw_f - rw_hi.astype(F32)).astype(BF16)], axis=1)
    rb = jnp.zeros((1, LANES), F32).at[0, :NE].set(router_bias.astype(F32))
    row2 = lambda v: v.reshape(1, -1)

    for l in range(depth):
        wl = w_in[l]
        wa = wl[:, :DC].astype(BF16)
        wg = wl[:, DC:2 * DC].astype(BF16)
        wqkvo = wl[:, 2 * DC:2 * DC + 4 * DML].astype(BF16)
        wgt = jnp.zeros((D, LANES), BF16).at[:, :2 * NH].set(wl[:, 2 * DC + 4 * DML:].astype(BF16))
        bl = b_in[l]
        bgt = jnp.zeros((1, LANES), F32).at[0, :2 * NH].set(bl[2 * DC + 4 * DML:])
        wc = w_out[l, :DC].astype(BF16)
        wh = w_out[l, DC:].astype(BF16)

        cnt = jnp.zeros((1, LANES), F32)
        for s in streams:
            B, L = s['B'], s['L']
            u, gates = _glu_proj(s['xb'], wa, wg, row2(bl[:DC]), row2(bl[DC:2 * DC]), wgt, bgt)
            qkvo = _mm_bias(s['xb'], wqkvo, row2(bl[2 * DC:2 * DC + 4 * DML]), BF16)
            if s['prompt']:
                hist = jnp.zeros((B, KW - 1, DC), F32)
                c0 = jnp.zeros((B, NH, dh, dh), F32)
                n0 = jnp.zeros((B, NH, 1, dh), F32)
                m0 = jnp.zeros((B, NH, 1, LANES), F32)
            else:
                hist = state_conv[l]
                c0 = state_C[l]
                n0 = state_n[l].reshape(B, NH, 1, dh)
                m0 = jnp.broadcast_to(state_m[l].reshape(B, NH, 1, 1), (B, NH, 1, LANES))
            c, new_hist = _conv_branch(u.reshape(B, L, DC), hist, conv_w[l], row2(conv_b[l]),
                                       row2(conv_ln_g[l]), row2(conv_ln_b[l]))
            gates_t = gates[:, :2 * NH].reshape(B, L, 2 * NH).transpose(0, 2, 1)
            hout, cN, nN, mN = _mlstm_branch(qkvo, gates, gates_t, c0, n0, m0, row2(ml_norm_g[l]), B, L)
            s['outs']['conv'].append(new_hist)
            s['outs']['C'].append(cN)
            s['outs']['n'].append(nN.reshape(B, NH, dh))
            s['outs']['m'].append(mN[:, :, 0, 0])
            x1, xp, info, cnt = _outproj_router(c.reshape(B * L, DC), hout, s['x'], wc, wh,
                                                row2(ln1_g[l]), row2(ln1_b[l]), rw, rb, cnt, alpha, NE)
            s['x1'], s['xp'], s['info'] = x1, xp, info

        dests, blk_e, blk_x, n_used = _dispatch_plan(cnt, [s['info'] for s in streams], NE, RB, NBLK)
        xb = jnp.zeros((P, D // 2), jnp.uint32)
        for s, dest in zip(streams, dests):
            xb = _scatter_rows(dest, s['xp'], xb)
        yb = _expert_ffn(l, blk_e, blk_x, n_used, xb, w_gate, w_up, w_down, RB)
        for s, dest in zip(streams, dests):
            s['x'], s['xb'] = _combine_ln(dest, s['x1'], s['info'], row2(ln2_g[l]), row2(ln2_b[l]), yb, alpha)

    outs = []
    for s in streams:
        outs.append(s['x'].reshape(s['B'], s['L'], D))
    for s in streams:
        o = s['outs']
        outs += [jnp.stack(o['conv']), jnp.stack(o['C']), jnp.stack(o['n']), jnp.stack(o['m'])]
    return tuple(outs)
```

```python
import functools

import jax
import jax.numpy as jnp
from jax import lax
from jax.experimental import pallas as pl
from jax.experimental.pallas import tpu as pltpu

F32 = jnp.float32
BF16 = jnp.bfloat16

N_GROUPS = 4
TOP_K = 2
LN_EPS = 1e-5
LANES = 128
SUBLANES = 8
VMEM_LIMIT = 56 * 1024 * 1024
DMA_UNROLL = 8
ROW_BLOCK = 256
ROUTE_SPLIT = 1


def _tile(n, target, align=SUBLANES):
    t = min(n, target)
    t -= t % align
    while t >= align:
        if n % t == 0:
            return t
        t -= align
    return n


def _params(*sem):
    return pltpu.CompilerParams(dimension_semantics=sem, vmem_limit_bytes=VMEM_LIMIT)


def _ln_rows(y, g, b):
    mu = jnp.mean(y, axis=-1, keepdims=True)
    d = y - mu
    var = jnp.mean(d * d, axis=-1, keepdims=True)
    return d * lax.rsqrt(var + LN_EPS) * g + b


def _log_sigmoid(x):
    return jnp.minimum(x, 0.0) - jnp.log1p(jnp.exp(-jnp.abs(x)))


def _glu_kernel(x_ref, wa_ref, wg_ref, ba_ref, bg_ref, wgt_ref, bgt_ref, u_ref, gates_ref):
    x = x_ref[...]
    a = jnp.dot(x, wa_ref[...], preferred_element_type=F32) + ba_ref[...]
    g = jnp.dot(x, wg_ref[...], preferred_element_type=F32) + bg_ref[...]
    u_ref[...] = a * jax.nn.sigmoid(g)

    @pl.when(pl.program_id(1) == 0)
    def _():
        gates_ref[...] = jnp.dot(x, wgt_ref[...], preferred_element_type=F32) + bgt_ref[...]


def _glu_proj(xb, wa, wg, ba, bg, wgt, bgt):
    T, D = xb.shape
    DC = wa.shape[1]
    TM = _tile(T, 1024)
    TN = _tile(DC, 512, LANES)
    return pl.pallas_call(
        _glu_kernel,
        grid=(T // TM, DC // TN),
        in_specs=[
            pl.BlockSpec((TM, D), lambda m, n: (m, 0)),
            pl.BlockSpec((D, TN), lambda m, n: (0, n)),
            pl.BlockSpec((D, TN), lambda m, n: (0, n)),
            pl.BlockSpec((1, TN), lambda m, n: (0, n)),
            pl.BlockSpec((1, TN), lambda m, n: (0, n)),
            pl.BlockSpec((D, LANES), lambda m, n: (0, 0)),
            pl.BlockSpec((1, LANES), lambda m, n: (0, 0)),
        ],
        out_specs=[
            pl.BlockSpec((TM, TN), lambda m, n: (m, n)),
            pl.BlockSpec((TM, LANES), lambda m, n: (m, 0)),
        ],
        out_shape=[jax.ShapeDtypeStruct((T, DC), F32), jax.ShapeDtypeStruct((T, LANES), F32)],
        compiler_params=_params("parallel", "arbitrary"),
        name="glu_proj",
    )(xb, wa, wg, ba, bg, wgt, bgt)


def _mm_kernel(x_ref, w_ref, b_ref, o_ref):
    acc = jnp.dot(x_ref[...], w_ref[...], preferred_element_type=F32) + b_ref[...]
    o_ref[...] = acc.astype(o_ref.dtype)


def _mm_bias(xb, w, b, out_dtype):
    T, D = xb.shape
    N = w.shape[1]
    TM = _tile(T, 1024)
    TN = _tile(N, 1024, LANES)
    return pl.pallas_call(
        _mm_kernel,
        grid=(T // TM, N // TN),
        in_specs=[
            pl.BlockSpec((TM, D), lambda m, n: (m, 0)),
            pl.BlockSpec((D, TN), lambda m, n: (0, n)),
            pl.BlockSpec((1, TN), lambda m, n: (0, n)),
        ],
        out_specs=pl.BlockSpec((TM, TN), lambda m, n: (m, n)),
        out_shape=jax.ShapeDtypeStruct((T, N), out_dtype),
        compiler_params=_params("parallel", "parallel"),
        name="qkvo_proj",
    )(xb, w, b)


def _conv_kernel(u_ref, hist_ref, w_ref, cb_ref, g_ref, b_ref, c_ref, nh_ref, ext, cbuf,
                 *, TT, KW, HP, RC, CC):
    t = pl.program_id(1)
    H = KW - 1
    DC = u_ref.shape[-1]

    @pl.when(t == 0)
    def _():
        ext[HP - H:HP, :] = hist_ref[...]

    ext[HP:HP + TT, :] = u_ref[...]

    WIN = RC + HP
    for r0 in range(0, TT, RC):
        for c0 in range(0, DC, CC):
            win = ext[r0:r0 + WIN, c0:c0 + CC]
            acc = None
            for sh in range(SUBLANES):
                shifted = win if sh == 0 else pltpu.roll(win, WIN - sh, axis=0)
                for s in range(sh, HP + 1, SUBLANES):
                    j = s - (HP - H)
                    if 0 <= j < KW:
                        term = w_ref[j:j + 1, c0:c0 + CC] * shifted[s - sh:s - sh + RC]
                        acc = term if acc is None else acc + term
            cbuf[r0:r0 + RC, c0:c0 + CC] = acc + cb_ref[:, c0:c0 + CC]

    y = _ln_rows(cbuf[...], g_ref[...], b_ref[...])
    c_ref[...] = (y * jax.nn.sigmoid(y)).astype(c_ref.dtype)

    tail = ext[HP + TT - H:HP + TT, :]
    ext[HP - H:HP, :] = tail

    @pl.when(t == pl.num_programs(1) - 1)
    def _():
        nh_ref[...] = tail


def _conv_branch(u, hist, conv_w, conv_b, g, b):
    B, L, DC = u.shape
    KW = conv_w.shape[0]
    TT = _tile(L, 512)
    assert TT >= KW - 1 and TT % SUBLANES == 0
    HP = -(-(KW - 1) // SUBLANES) * SUBLANES
    RC = _tile(TT, 128)
    CC = LANES
    kern = functools.partial(_conv_kernel, TT=TT, KW=KW, HP=HP, RC=RC, CC=CC)
    return pl.pallas_call(
        kern,
        grid=(B, L // TT),
        in_specs=[
            pl.BlockSpec((None, TT, DC), lambda bi, t: (bi, t, 0)),
            pl.BlockSpec((None, KW - 1, DC), lambda bi, t: (bi, 0, 0)),
            pl.BlockSpec((KW, DC), lambda bi, t: (0, 0)),
            pl.BlockSpec((1, DC), lambda bi, t: (0, 0)),
            pl.BlockSpec((1, DC), lambda bi, t: (0, 0)),
            pl.BlockSpec((1, DC), lambda bi, t: (0, 0)),
        ],
        out_specs=[
            pl.BlockSpec((None, TT, DC), lambda bi, t: (bi, t, 0)),
            pl.BlockSpec((None, KW - 1, DC), lambda bi, t: (bi, 0, 0)),
        ],
        out_shape=[jax.ShapeDtypeStruct((B, L, DC), BF16), jax.ShapeDtypeStruct((B, KW - 1, DC), F32)],
        scratch_shapes=[pltpu.VMEM((HP + TT, DC), F32), pltpu.VMEM((TT, DC), F32)],
        compiler_params=_params("parallel", "arbitrary"),
        name="conv_branch",
    )(u, hist, conv_w, conv_b, g, b)


def _mlstm_kernel(q_ref, k_ref, v_ref, o_ref, gc_ref, gr_ref, c0_ref, n0_ref, m0_ref, mg_ref,
                  h_ref, cN_ref, nN_ref, mN_ref, c_scr, n_scr, m_scr, *, NH, LC):
    ck = pl.program_id(1)
    dh = c_scr.shape[-1]
    scale = dh ** -0.5

    @pl.when(ck == 0)
    def _():
        c_scr[...] = c0_ref[...]
        n_scr[...] = n0_ref[...]
        m_scr[...] = m0_ref[...]

    gcol = gc_ref[...]
    lane = lax.broadcasted_iota(jnp.int32, gcol.shape, 1)
    grow = gr_ref[...]
    ti = lax.broadcasted_iota(jnp.int32, (LC, LC), 0)
    si = lax.broadcasted_iota(jnp.int32, (LC, LC), 1)
    causal = si <= ti
    anti = ti <= si

    for hd in range(NH):
        cols = slice(hd * dh, (hd + 1) * dh)
        li_col = jnp.sum(jnp.where(lane == hd, gcol, 0.0), axis=1, keepdims=True)
        lf_col = _log_sigmoid(jnp.sum(jnp.where(lane == NH + hd, gcol, 0.0), axis=1, keepdims=True))
        li_row = grow[hd:hd + 1, :]
        lf_row = _log_sigmoid(grow[NH + hd:NH + hd + 1, :])

        f_col = jnp.sum(jnp.where(causal, lf_row, 0.0), axis=1, keepdims=True)
        f_row = jnp.sum(jnp.where(anti, lf_col, 0.0), axis=0, keepdims=True)
        g_row = li_row - f_row
        g_col = li_col - f_col

        m_prev = m_scr[hd, :, 0:1]
        a = jnp.where(causal, g_row, -jnp.inf)
        m_col = jnp.maximum(m_prev, jnp.max(a, axis=1, keepdims=True))
        dmat = jnp.exp(a - m_col)
        inter = jnp.exp(m_prev - m_col)

        q = q_ref[:, cols]
        k = k_ref[:, cols]
        v = v_ref[:, cols]
        c_prev = c_scr[hd]
        n_prev = n_scr[hd]
        s = lax.dot_general(q, k, (((1,), (1,)), ((), ())), preferred_element_type=F32) * scale * dmat
        num = (jnp.dot(s.astype(BF16), v, preferred_element_type=F32)
               + inter * jnp.dot(q, c_prev.astype(BF16), preferred_element_type=F32))
        den = (jnp.sum(s, axis=1, keepdims=True)
               + inter * jnp.sum(q.astype(F32) * n_prev, axis=1, keepdims=True))
        hcore = num / jnp.maximum(jnp.abs(den), jnp.exp(-(f_col + m_col)))

        mu = jnp.mean(hcore, axis=1, keepdims=True)
        d = hcore - mu
        var = jnp.mean(d * d, axis=1, keepdims=True)
        hn = d * lax.rsqrt(var + LN_EPS)
        gate = jax.nn.sigmoid(o_ref[:, cols].astype(F32))
        h_ref[:, cols] = (gate * hn * mg_ref[:, cols]).astype(h_ref.dtype)

        m_last = m_col[LC - 1:LC, :]
        f_tot = f_col[LC - 1:LC, :]
        decay = jnp.exp(m_prev - m_last)
        kw = k.astype(F32) * (jnp.exp(g_col - m_last) * scale)
        c_new = decay * c_prev + lax.dot_general(kw.astype(BF16), v, (((0,), (0,)), ((), ())),
                                                 preferred_element_type=F32)
        n_new = decay * n_prev + jnp.sum(kw, axis=0, keepdims=True)
        m_new = jnp.broadcast_to(f_tot + m_last, (1, LANES))
        c_scr[hd] = c_new
        n_scr[hd] = n_new
        m_scr[hd] = m_new
        cN_ref[hd] = c_new
        nN_ref[hd] = n_new
        mN_ref[hd] = m_new


def _mlstm_branch(qkvo, gates, gates_t, c0, n0, m0, mln_g, B, L):
    NH, dh = c0.shape[1], c0.shape[2]
    DML = NH * dh
    LC = _tile(L, 256, LANES) if L % LANES == 0 else L
    NC = L // LC
    kern = functools.partial(_mlstm_kernel, NH=NH, LC=LC)
    qkv_spec = lambda off: pl.BlockSpec((LC, DML), lambda b, c: (b * NC + c, off))
    state4 = lambda r, w: pl.BlockSpec((None, NH, r, w), lambda b, c: (b, 0, 0, 0))
    return pl.pallas_call(
        kern,
        grid=(B, NC),
        in_specs=[
            qkv_spec(0), qkv_spec(1), qkv_spec(2), qkv_spec(3),
            pl.BlockSpec((LC, LANES), lambda b, c: (b * NC + c, 0)),
            pl.BlockSpec((None, 2 * NH, LC), lambda b, c: (b, 0, c)),
            state4(dh, dh), state4(1, dh), state4(1, LANES),
            pl.BlockSpec((1, DML), lambda b, c: (0, 0)),
        ],
        out_specs=[
            pl.BlockSpec((LC, DML), lambda b, c: (b * NC + c, 0)),
            state4(dh, dh), state4(1, dh), state4(1, LANES),
        ],
        out_shape=[
            jax.ShapeDtypeStruct((B * L, DML), BF16),
            jax.ShapeDtypeStruct((B, NH, dh, dh), F32),
            jax.ShapeDtypeStruct((B, NH, 1, dh), F32),
            jax.ShapeDtypeStruct((B, NH, 1, LANES), F32),
        ],
        scratch_shapes=[pltpu.VMEM((NH, dh, dh), F32), pltpu.VMEM((NH, 1, dh), F32),
                        pltpu.VMEM((NH, 1, LANES), F32)],
        compiler_params=_params("parallel", "arbitrary"),
        name="mlstm_branch",
    )(qkvo, qkvo, qkvo, qkvo, gates, gates_t, c0, n0, m0, mln_g)


def _outproj_kernel(c_ref, h_ref, x_ref, wc_ref, wh_ref, g_ref, b_ref, rw_ref, rb_ref, cnt0_ref,
                    x1_ref, xp_ref, info_ref, cnt_ref, run_scr, *, alpha, NE):
    i = pl.program_id(0)
    TM, D = x_ref.shape
    half = D // 2
    epg = NE // N_GROUPS
    neg = -jnp.inf

    @pl.when(i == 0)
    def _():
        run_scr[...] = cnt0_ref[...]

    SM = TM // ROUTE_SPLIT if TM % (ROUTE_SPLIT * 2 * SUBLANES) == 0 else TM
    lane = lax.broadcasted_iota(jnp.int32, (SM, LANES), 1)
    lanef = lane.astype(F32)
    ri = lax.broadcasted_iota(jnp.int32, (SM, SM), 0)
    ci = lax.broadcasted_iota(jnp.int32, (SM, SM), 1)
    tri = jnp.where(ci < ri, 1.0, 0.0).astype(BF16)
    run = run_scr[...]

    for r0_ in range(0, TM, SM):
        rs = slice(r0_, r0_ + SM)
        mix = (jnp.dot(c_ref[rs, :], wc_ref[...], preferred_element_type=F32)
               + jnp.dot(h_ref[rs, :], wh_ref[...], preferred_element_type=F32))
        x1 = _ln_rows(alpha * x_ref[rs, :] + mix, g_ref[...], b_ref[...])
        x1_ref[rs, :] = x1

        x1h = x1.astype(BF16)
        x1hf = x1h.astype(F32)
        lo = lax.bitcast_convert_type(x1hf[:, :half], jnp.uint32)
        hi = lax.bitcast_convert_type(x1hf[:, half:], jnp.uint32)
        xp_ref[rs, :] = (hi & jnp.uint32(0xFFFF0000)) | (lo >> 16)

        x1l = (x1 - x1hf).astype(BF16)
        p = jnp.dot(x1h, rw_ref[...], preferred_element_type=F32)
        logits = (p[:, :LANES] + p[:, LANES:]
                  + jnp.dot(x1l, rw_ref[:, :LANES], preferred_element_type=F32))
        aff = jax.nn.sigmoid(logits)
        biased = jnp.where(lane < NE, aff + rb_ref[...], neg)
        best = e0 = e1 = None
        for grp in range(N_GROUPS):
            vg = jnp.where((lane >= grp * epg) & (lane < (grp + 1) * epg), biased, neg)
            m1 = jnp.max(vg, axis=1, keepdims=True)
            i1 = jnp.min(jnp.where(vg == m1, lanef, float(LANES)), axis=1, keepdims=True)
            vg2 = jnp.where(lanef == i1, neg, vg)
            m2 = jnp.max(vg2, axis=1, keepdims=True)
            i2 = jnp.min(jnp.where(vg2 == m2, lanef, float(LANES)), axis=1, keepdims=True)
            score = m1 + m2
            if grp == 0:
                best, e0, e1 = score, i1, i2
            else:
                better = score > best
                best = jnp.where(better, score, best)
                e0 = jnp.where(better, i1, e0)
                e1 = jnp.where(better, i2, e1)

        oh0 = lanef == e0
        oh1 = lanef == e1
        a0 = jnp.sum(jnp.where(oh0, aff, 0.0), axis=1, keepdims=True)
        a1 = jnp.sum(jnp.where(oh1, aff, 0.0), axis=1, keepdims=True)
        asum = a0 + a1
        onehot = jnp.where(oh0 | oh1, 1.0, 0.0)
        prefix = jnp.dot(tri, onehot.astype(BF16), preferred_element_type=F32) + run
        rk0 = jnp.sum(jnp.where(oh0, prefix, 0.0), axis=1, keepdims=True)
        rk1 = jnp.sum(jnp.where(oh1, prefix, 0.0), axis=1, keepdims=True)
        run = run + jnp.sum(onehot, axis=0, keepdims=True)

        cols = (e0, e1, rk0, rk1, a0 / asum, a1 / asum)
        info = jnp.zeros((SM, LANES), F32)
        for ci_, val in enumerate(cols):
            info = jnp.where(lane == ci_, val, info)
        info_ref[rs, :] = info

    run_scr[...] = run
    cnt_ref[...] = run


def _outproj_router(c, h, x, wc, wh, g, b, rw, rb, cnt0, alpha, NE):
    T, D = x.shape
    DC, DML = c.shape[1], h.shape[1]
    TM = _tile(T, 512)
    kern = functools.partial(_outproj_kernel, alpha=alpha, NE=NE)
    rowspec = lambda w: pl.BlockSpec((TM, w), lambda i: (i, 0))
    full = lambda r, w: pl.BlockSpec((r, w), lambda i: (0, 0))
    return pl.pallas_call(
        kern,
        grid=(T // TM,),
        in_specs=[rowspec(DC), rowspec(DML), rowspec(D), full(DC, D), full(DML, D), full(1, D), full(1, D),
                  full(D, 2 * LANES), full(1, LANES), full(1, LANES)],
        out_specs=[rowspec(D), rowspec(D // 2), rowspec(LANES), full(1, LANES)],
        out_shape=[
            jax.ShapeDtypeStruct((T, D), F32),
            jax.ShapeDtypeStruct((T, D // 2), jnp.uint32),
            jax.ShapeDtypeStruct((T, LANES), F32),
            jax.ShapeDtypeStruct((1, LANES), F32),
        ],
        scratch_shapes=[pltpu.VMEM((1, LANES), F32)],
        compiler_params=_params("arbitrary"),
        name="outproj_router",
    )(c, h, x, wc, wh, g, b, rw, rb, cnt0)


def _scatter_kernel(dest_ref, pend_ref, x_ref, *rest, TMS, RB, NE, first):
    if first:
        xb_out, zeros, sem = rest

        @pl.when(pl.program_id(0) == 0)
        def _():
            zeros[...] = jnp.zeros(zeros.shape, zeros.dtype)

            def pad_copy(e):
                start = pl.multiple_of(pend_ref[e] - RB, RB)
                return pltpu.make_async_copy(zeros, xb_out.at[pl.ds(start, RB)], sem)

            def has_rows(e):
                return pend_ref[e] > (pend_ref[e - 1] if e > 0 else 0)

            for e in range(NE):
                @pl.when(has_rows(e))
                def _():
                    pad_copy(e).start()

            for e in range(NE):
                @pl.when(has_rows(e))
                def _():
                    pad_copy(e).wait()
    else:
        _, xb_out, sem = rest

    def copy(r, kk):
        return pltpu.make_async_copy(x_ref.at[pl.ds(r, 1)],
                                     xb_out.at[pl.ds(dest_ref[TOP_K * r + kk], 1)], sem)

    def issue(r, carry):
        for kk in range(TOP_K):
            copy(r, kk).start()
        return carry

    lax.fori_loop(0, TMS, issue, 0, unroll=DMA_UNROLL)

    def drain(r, carry):
        for kk in range(TOP_K):
            copy(r, kk).wait()
        return carry

    lax.fori_loop(0, TMS, drain, 0, unroll=DMA_UNROLL)


def _scatter_rows(dest_flat, pend, xp, xb, P, RB):
    T, half = xp.shape
    NE = pend.shape[0]
    TMS = _tile(T, 512)
    first = xb is None
    kern = functools.partial(_scatter_kernel, TMS=TMS, RB=RB, NE=NE, first=first)
    in_specs = [
        pl.BlockSpec((TOP_K * TMS,), lambda i: (i,), memory_space=pltpu.SMEM),
        pl.BlockSpec((NE,), lambda i: (0,), memory_space=pltpu.SMEM),
        pl.BlockSpec((TMS, half), lambda i: (i, 0)),
    ]
    scratch = [pltpu.SemaphoreType.DMA(())]
    args = [dest_flat, pend, xp]
    if first:
        scratch = [pltpu.VMEM((RB, half), xp.dtype)] + scratch
    else:
        in_specs.append(pl.BlockSpec(memory_space=pl.ANY))
        args.append(xb)
    return pl.pallas_call(
        kern,
        grid=(T // TMS,),
        in_specs=in_specs,
        out_specs=pl.BlockSpec(memory_space=pl.ANY),
        out_shape=jax.ShapeDtypeStruct((P, half), xp.dtype),
        scratch_shapes=scratch,
        input_output_aliases={} if first else {3: 0},
        compiler_params=_params("arbitrary"),
        name="moe_scatter",
    )(*args)


def _new_expert(i, blk_e):
    return (i == 0) | (blk_e[i] != blk_e[jnp.maximum(i - 1, 0)])


def _pack_halves(x):
    h = x.shape[-1] // 2
    lo = lax.bitcast_convert_type(x[:, :h].astype(BF16).astype(F32), jnp.uint32)
    hi = lax.bitcast_convert_type(x[:, h:].astype(BF16).astype(F32), jnp.uint32)
    return (hi & jnp.uint32(0xFFFF0000)) | (lo >> 16)


def _unpack_halves(w):
    lo = lax.bitcast_convert_type(w << 16, F32)
    hi = lax.bitcast_convert_type(w & jnp.uint32(0xFFFF0000), F32)
    return lo, hi


def _expert_kernel(blk_e, blk_x, nxt_e, n_used, xb_ref, wg_hbm, wu_hbm, wd_hbm, y_ref,
                   stg, stu, std, wgs, wus, wds, sems, *, layer):
    del blk_x
    i = pl.program_id(0)
    half = xb_ref.shape[-1]

    def weight_copies(e):
        return (pltpu.make_async_copy(wg_hbm.at[layer, e], stg, sems.at[0]),
                pltpu.make_async_copy(wu_hbm.at[layer, e], stu, sems.at[1]),
                pltpu.make_async_copy(wd_hbm.at[layer, e], std, sems.at[2]))

    @pl.when(i < n_used[0])
    def _():
        @pl.when(_new_expert(i, blk_e))
        def _():
            @pl.when(i == 0)
            def _():
                for cp in weight_copies(blk_e[i]):
                    cp.start()

            for cp in weight_copies(blk_e[i]):
                cp.wait()
            wgs[...] = stg[...].astype(BF16)
            wus[...] = stu[...].astype(BF16)
            wds[...] = std[...].astype(BF16)

            @pl.when(nxt_e[i] >= 0)
            def _():
                for cp in weight_copies(nxt_e[i]):
                    cp.start()

        lo, hi = _unpack_halves(xb_ref[...])
        lo = lo.astype(BF16)
        hi = hi.astype(BF16)
        g = (jnp.dot(lo, wgs[0:half, :], preferred_element_type=F32)
             + jnp.dot(hi, wgs[half:2 * half, :], preferred_element_type=F32))
        u = (jnp.dot(lo, wus[0:half, :], preferred_element_type=F32)
             + jnp.dot(hi, wus[half:2 * half, :], preferred_element_type=F32))
        hid = (g * jax.nn.sigmoid(g) * u).astype(BF16)
        y_ref[...] = _pack_halves(jnp.dot(hid, wds[...], preferred_element_type=F32))


def _expert_ffn(layer, blk_e, blk_x, nxt_e, n_used, xb, w_gate, w_up, w_down, RB):
    P, half = xb.shape
    D = 2 * half
    DFF = w_gate.shape[-1]
    rows = pl.BlockSpec((RB, half), lambda i, be, bx, nx, nu: (bx[i], 0))
    anyspec = pl.BlockSpec(memory_space=pl.ANY)
    return pl.pallas_call(
        functools.partial(_expert_kernel, layer=layer),
        grid_spec=pltpu.PrefetchScalarGridSpec(
            num_scalar_prefetch=4,
            grid=(P // RB,),
            in_specs=[rows, anyspec, anyspec, anyspec],
            out_specs=rows,
            scratch_shapes=[pltpu.VMEM((D, DFF), F32), pltpu.VMEM((D, DFF), F32), pltpu.VMEM((DFF, D), F32),
                            pltpu.VMEM((D, DFF), BF16), pltpu.VMEM((D, DFF), BF16), pltpu.VMEM((DFF, D), BF16),
                            pltpu.SemaphoreType.DMA((3,))],
        ),
        out_shape=jax.ShapeDtypeStruct((P, half), jnp.uint32),
        compiler_params=_params("arbitrary"),
        name="expert_ffn",
    )(blk_e, blk_x, nxt_e, n_used, xb, w_gate, w_up, w_down)


def _combine_kernel(dest_ref, x1_ref, info_ref, g_ref, b_ref, yb_hbm, x2_ref, x2b_ref, buf, sem,
                    *, alpha, TMC):
    def copy(r, kk):
        return pltpu.make_async_copy(yb_hbm.at[pl.ds(dest_ref[TOP_K * r + kk], 1)],
                                     buf.at[kk, pl.ds(r, 1)], sem)

    def issue(r, carry):
        for kk in range(TOP_K):
            copy(r, kk).start()
        return carry

    lax.fori_loop(0, TMC, issue, 0, unroll=DMA_UNROLL)

    def drain(r, carry):
        for kk in range(TOP_K):
            copy(r, kk).wait()
        return carry

    lax.fori_loop(0, TMC, drain, 0, unroll=DMA_UNROLL)

    info = info_ref[...]
    g0 = info[:, 4:5]
    g1 = info[:, 5:6]
    half = buf.shape[-1]
    lo0, hi0 = _unpack_halves(buf[0])
    lo1, hi1 = _unpack_halves(buf[1])
    zl = alpha * x1_ref[:, 0:half] + (g0 * lo0 + g1 * lo1)
    zh = alpha * x1_ref[:, half:2 * half] + (g0 * hi0 + g1 * hi1)
    inv_d = 1.0 / (2 * half)
    mu = (jnp.sum(zl, axis=-1, keepdims=True) + jnp.sum(zh, axis=-1, keepdims=True)) * inv_d
    dl = zl - mu
    dh_ = zh - mu
    var = (jnp.sum(dl * dl, axis=-1, keepdims=True) + jnp.sum(dh_ * dh_, axis=-1, keepdims=True)) * inv_d
    rs = lax.rsqrt(var + LN_EPS)
    xl = dl * rs * g_ref[:, 0:half] + b_ref[:, 0:half]
    xh = dh_ * rs * g_ref[:, half:2 * half] + b_ref[:, half:2 * half]
    x2_ref[:, 0:half] = xl
    x2_ref[:, half:2 * half] = xh
    x2b_ref[:, 0:half] = xl.astype(BF16)
    x2b_ref[:, half:2 * half] = xh.astype(BF16)


def _combine_ln(dest_flat, x1, info, g, b, yb, alpha):
    T, D = x1.shape
    TMC = _tile(T, 512)
    kern = functools.partial(_combine_kernel, alpha=alpha, TMC=TMC)
    rowspec = lambda w: pl.BlockSpec((TMC, w), lambda i: (i, 0))
    return pl.pallas_call(
        kern,
        grid=(T // TMC,),
        in_specs=[
            pl.BlockSpec((TOP_K * TMC,), lambda i: (i,), memory_space=pltpu.SMEM),
            rowspec(D), rowspec(LANES),
            pl.BlockSpec((1, D), lambda i: (0, 0)), pl.BlockSpec((1, D), lambda i: (0, 0)),
            pl.BlockSpec(memory_space=pl.ANY),
        ],
        out_specs=[rowspec(D), rowspec(D)],
        out_shape=[jax.ShapeDtypeStruct((T, D), F32), jax.ShapeDtypeStruct((T, D), BF16)],
        scratch_shapes=[pltpu.VMEM((TOP_K, TMC, D // 2), jnp.uint32), pltpu.SemaphoreType.DMA(())],
        compiler_params=_params("arbitrary"),
        name="moe_combine",
    )(dest_flat, x1, info, g, b, yb)


def _dispatch_plan(cnt, infos, NE, RB, NBLK):
    counts = cnt[0, :NE].astype(jnp.int32)
    pcounts = (counts + RB - 1) // RB * RB
    pend = jnp.cumsum(pcounts)
    pstart = pend - pcounts
    dests = []
    for info in infos:
        e = info[:, 0:TOP_K].astype(jnp.int32)
        rank = info[:, TOP_K:2 * TOP_K].astype(jnp.int32)
        onehot = (e[..., None] == jnp.arange(NE, dtype=jnp.int32)).astype(jnp.int32)
        dests.append((jnp.sum(onehot * pstart, axis=-1) + rank).reshape(-1))
    n_used = jnp.maximum(pend[-1] // RB, 1).astype(jnp.int32)
    blk_x = jnp.minimum(jnp.arange(NBLK, dtype=jnp.int32), n_used - 1)
    blk_e = jnp.sum((pend[None, :] <= (blk_x * RB)[:, None]).astype(jnp.int32), axis=1)
    blk_e = jnp.minimum(blk_e, NE - 1)
    ids = jnp.arange(NE, dtype=jnp.int32)
    later = (ids[None, :] > ids[:, None]) & (counts > 0)[None, :]
    nxt = jnp.min(jnp.where(later, ids[None, :], NE), axis=1)
    nxt = jnp.where(nxt == NE, -1, nxt)
    nxt_e = jnp.sum((blk_e[:, None] == ids[None, :]).astype(jnp.int32) * nxt[None, :], axis=1)
    return dests, pend.astype(jnp.int32), blk_e, blk_x, nxt_e.astype(jnp.int32), n_used.reshape(1)


def kernel(x_prompt, x_sample, state_conv, state_C, state_n, state_m, w_in, b_in, conv_w, conv_b,
           conv_ln_g, conv_ln_b, ml_norm_g, w_out, ln1_g, ln1_b, router_w, router_bias,
           w_gate, w_up, w_down, ln2_g, ln2_b):
    depth, D, _ = w_in.shape
    DC = conv_w.shape[-1]
    NH, dh = state_C.shape[2], state_C.shape[3]
    DML = NH * dh
    NE = router_w.shape[1]
    alpha = (2.0 * depth) ** 0.25
    KW = conv_w.shape[1]

    streams = []
    Bp, Lp, _ = x_prompt.shape
    Bs, Ls, _ = x_sample.shape
    streams.append(dict(B=Bp, L=Lp, x=x_prompt.reshape(Bp * Lp, D), prompt=True))
    streams.append(dict(B=Bs, L=Ls, x=x_sample.reshape(Bs * Ls, D), prompt=False))
    for s in streams:
        s['xb'] = s['x'].astype(BF16)
        s['outs'] = dict(conv=[], C=[], n=[], m=[])

    A = TOP_K * (Bp * Lp + Bs * Ls)
    RB = ROW_BLOCK if A >= 32 * ROW_BLOCK else 32
    NBLK = -(-A // RB) + NE
    P = NBLK * RB

    rw_f = jnp.zeros((D, LANES), F32).at[:, :NE].set(router_w.astype(F32))
    rw_hi = rw_f.astype(BF16)
    rw = jnp.concatenate([rw_hi, (rw_f - rw_hi.astype(F32)).astype(BF16)], axis=1)
    rb = jnp.zeros((1, LANES), F32).at[0, :NE].set(router_bias.astype(F32))
    row2 = lambda v: v.reshape(1, -1)

    for l in range(depth):
        wl = w_in[l]
        wa = wl[:, :DC].astype(BF16)
        wg = wl[:, DC:2 * DC].astype(BF16)
        wqkvo = wl[:, 2 * DC:2 * DC + 4 * DML].astype(BF16)
        wgt = jnp.zeros((D, LANES), BF16).at[:, :2 * NH].set(wl[:, 2 * DC + 4 * DML:].astype(BF16))
        bl = b_in[l]
        bgt = jnp.zeros((1, LANES), F32).at[0, :2 * NH].set(bl[2 * DC + 4 * DML:])
        wc = w_out[l, :DC].astype(BF16)
        wh = w_out[l, DC:].astype(BF16)

        cnt = jnp.zeros((1, LANES), F32)
        for s in streams:
            B, L = s['B'], s['L']
            u, gates = _glu_proj(s['xb'], wa, wg, row2(bl[:DC]), row2(bl[DC:2 * DC]), wgt, bgt)
            qkvo = _mm_bias(s['xb'], wqkvo, row2(bl[2 * DC:2 * DC + 4 * DML]), BF16)
            if s['prompt']:
                hist = jnp.zeros((B, KW - 1, DC), F32)
                c0 = jnp.zeros((B, NH, dh, dh), F32)
                n0 = jnp.zeros((B, NH, 1, dh), F32)
                m0 = jnp.zeros((B, NH, 1, LANES), F32)
            else:
                hist = state_conv[l]
                c0 = state_C[l]
                n0 = state_n[l].reshape(B, NH, 1, dh)
                m0 = jnp.broadcast_to(state_m[l].reshape(B, NH, 1, 1), (B, NH, 1, LANES))
            c, new_hist = _conv_branch(u.reshape(B, L, DC), hist, conv_w[l], row2(conv_b[l]),
                                       row2(conv_ln_g[l]), row2(conv_ln_b[l]))
            gates_t = gates[:, :2 * NH].reshape(B, L, 2 * NH).transpose(0, 2, 1)
            hout, cN, nN, mN = _mlstm_branch(qkvo, gates, gates_t, c0, n0, m0, row2(ml_norm_g[l]), B, L)
            s['outs']['conv'].append(new_hist)
            s['outs']['C'].append(cN)
            s['outs']['n'].append(nN.reshape(B, NH, dh))
            s['outs']['m'].append(mN[:, :, 0, 0])
            x1, xp, info, cnt = _outproj_router(c.reshape(B * L, DC), hout, s['x'], wc, wh,
                                                row2(ln1_g[l]), row2(ln1_b[l]), rw, rb, cnt, alpha, NE)
            s['x1'], s['xp'], s['info'] = x1, xp, info

        dests, pend, blk_e, blk_x, nxt_e, n_used = _dispatch_plan(
            cnt, [s['info'] for s in streams], NE, RB, NBLK)
        xb = None
        for s, dest in zip(streams, dests):
            xb = _scatter_rows(dest, pend, s['xp'], xb, P, RB)
        yb = _expert_ffn(l, blk_e, blk_x, nxt_e, n_used, xb, w_gate, w_up, w_down, RB)
        for s, dest in zip(streams, dests):
            s['x'], s['xb'] = _combine_ln(dest, s['x1'], s['info'], row2(ln2_g[l]), row2(ln2_b[l]), yb, alpha)

    outs = []
    for s in streams:
        outs.append(s['x'].reshape(s['B'], s['L'], D))
    for s in streams:
        o = s['outs']
        outs += [jnp.stack(o['conv']), jnp.stack(o['C']), jnp.stack(o['n']), jnp.stack(o['m'])]
    return tuple(outs)
```

```python
import functools

import jax
import jax.numpy as jnp
from jax import lax
from jax.experimental import pallas as pl
from jax.experimental.pallas import tpu as pltpu

F32 = jnp.float32
BF16 = jnp.bfloat16

N_GROUPS = 4
TOP_K = 2
LN_EPS = 1e-5
LANES = 128
SUBLANES = 8
VMEM_LIMIT = 56 * 1024 * 1024
DMA_UNROLL = 8
ROW_BLOCK = 256
ROUTE_SPLIT = 1


def _tile(n, target, align=SUBLANES):
    t = min(n, target)
    t -= t % align
    while t >= align:
        if n % t == 0:
            return t
        t -= align
    return n


def _params(*sem):
    return pltpu.CompilerParams(dimension_semantics=sem, vmem_limit_bytes=VMEM_LIMIT)


def _ln_rows(y, g, b):
    mu = jnp.mean(y, axis=-1, keepdims=True)
    d = y - mu
    var = jnp.mean(d * d, axis=-1, keepdims=True)
    return d * lax.rsqrt(var + LN_EPS) * g + b


def _log_sigmoid(x):
    return jnp.minimum(x, 0.0) - jnp.log1p(jnp.exp(-jnp.abs(x)))


def _load_tiled_rows(ref, n, nch, row0=0):
    return jnp.concatenate([ref[pl.ds(row0 * nch + c, n, stride=nch), :] for c in range(nch)], axis=1)


def _store_tiled_rows(ref, val, nch, row0=0):
    n = val.shape[0]
    for c in range(nch):
        ref[pl.ds(row0 * nch + c, n, stride=nch), :] = val[:, c * LANES:(c + 1) * LANES]


def _glu_kernel(x_ref, wa_ref, wg_ref, ba_ref, bg_ref, wgt_ref, bgt_ref, u_ref, gates_ref, *xb_ref):
    if xb_ref:
        @pl.when(pl.program_id(1) == 0)
        def _():
            xb_ref[0][...] = x_ref[...].astype(BF16)

        x = xb_ref[0][...]
    else:
        x = x_ref[...]
    a = jnp.dot(x, wa_ref[...], preferred_element_type=F32) + ba_ref[...]
    g = jnp.dot(x, wg_ref[...], preferred_element_type=F32) + bg_ref[...]
    u_ref[...] = a * jax.nn.sigmoid(g)

    @pl.when(pl.program_id(1) == 0)
    def _():
        gates_ref[...] = jnp.dot(x, wgt_ref[...], preferred_element_type=F32) + bgt_ref[...]


def _glu_proj(xb, wa, wg, ba, bg, wgt, bgt):
    T, D = xb.shape
    DC = wa.shape[1]
    TM = _tile(T, 1024)
    TN = _tile(DC, 512, LANES)
    out_specs = [
        pl.BlockSpec((TM, TN), lambda m, n: (m, n)),
        pl.BlockSpec((TM, LANES), lambda m, n: (m, 0)),
    ]
    out_shape = [jax.ShapeDtypeStruct((T, DC), F32), jax.ShapeDtypeStruct((T, LANES), F32)]
    if xb.dtype != BF16:
        out_specs.append(pl.BlockSpec((TM, D), lambda m, n: (m, 0)))
        out_shape.append(jax.ShapeDtypeStruct((T, D), BF16))
    return pl.pallas_call(
        _glu_kernel,
        grid=(T // TM, DC // TN),
        in_specs=[
            pl.BlockSpec((TM, D), lambda m, n: (m, 0)),
            pl.BlockSpec((D, TN), lambda m, n: (0, n)),
            pl.BlockSpec((D, TN), lambda m, n: (0, n)),
            pl.BlockSpec((1, TN), lambda m, n: (0, n)),
            pl.BlockSpec((1, TN), lambda m, n: (0, n)),
            pl.BlockSpec((D, LANES), lambda m, n: (0, 0)),
            pl.BlockSpec((1, LANES), lambda m, n: (0, 0)),
        ],
        out_specs=out_specs,
        out_shape=out_shape,
        compiler_params=_params("parallel", "arbitrary"),
        name="glu_proj",
    )(xb, wa, wg, ba, bg, wgt, bgt)


def _mm_kernel(x_ref, w_ref, b_ref, o_ref):
    acc = jnp.dot(x_ref[...], w_ref[...], preferred_element_type=F32) + b_ref[...]
    o_ref[...] = acc.astype(o_ref.dtype)


def _mm_bias(xb, w, b, out_dtype):
    T, D = xb.shape
    N = w.shape[1]
    TM = _tile(T, 1024)
    TN = _tile(N, 1024, LANES)
    return pl.pallas_call(
        _mm_kernel,
        grid=(T // TM, N // TN),
        in_specs=[
            pl.BlockSpec((TM, D), lambda m, n: (m, 0)),
            pl.BlockSpec((D, TN), lambda m, n: (0, n)),
            pl.BlockSpec((1, TN), lambda m, n: (0, n)),
        ],
        out_specs=pl.BlockSpec((TM, TN), lambda m, n: (m, n)),
        out_shape=jax.ShapeDtypeStruct((T, N), out_dtype),
        compiler_params=_params("parallel", "parallel"),
        name="qkvo_proj",
    )(xb, w, b)


def _conv_kernel(u_ref, hist_ref, w_ref, cb_ref, g_ref, b_ref, c_ref, nh_ref, ext, cbuf,
                 *, TT, KW, HP, RC, CC):
    t = pl.program_id(1)
    H = KW - 1
    DC = u_ref.shape[-1]

    @pl.when(t == 0)
    def _():
        ext[HP - H:HP, :] = hist_ref[...]

    ext[HP:HP + TT, :] = u_ref[...]

    WIN = RC + HP
    for r0 in range(0, TT, RC):
        for c0 in range(0, DC, CC):
            win = ext[r0:r0 + WIN, c0:c0 + CC]
            acc = None
            for sh in range(SUBLANES):
                shifted = win if sh == 0 else pltpu.roll(win, WIN - sh, axis=0)
                for s in range(sh, HP + 1, SUBLANES):
                    j = s - (HP - H)
                    if 0 <= j < KW:
                        term = w_ref[j:j + 1, c0:c0 + CC] * shifted[s - sh:s - sh + RC]
                        acc = term if acc is None else acc + term
            cbuf[r0:r0 + RC, c0:c0 + CC] = acc + cb_ref[:, c0:c0 + CC]

    y = _ln_rows(cbuf[...], g_ref[...], b_ref[...])
    c_ref[...] = (y * jax.nn.sigmoid(y)).astype(c_ref.dtype)

    tail = ext[HP + TT - H:HP + TT, :]
    ext[HP - H:HP, :] = tail

    @pl.when(t == pl.num_programs(1) - 1)
    def _():
        nh_ref[...] = tail


def _conv_branch(u, hist, conv_w, conv_b, g, b):
    B, L, DC = u.shape
    KW = conv_w.shape[0]
    TT = _tile(L, 512)
    assert TT >= KW - 1 and TT % SUBLANES == 0
    HP = -(-(KW - 1) // SUBLANES) * SUBLANES
    RC = _tile(TT, 128)
    CC = LANES
    kern = functools.partial(_conv_kernel, TT=TT, KW=KW, HP=HP, RC=RC, CC=CC)
    return pl.pallas_call(
        kern,
        grid=(B, L // TT),
        in_specs=[
            pl.BlockSpec((None, TT, DC), lambda bi, t: (bi, t, 0)),
            pl.BlockSpec((None, KW - 1, DC), lambda bi, t: (bi, 0, 0)),
            pl.BlockSpec((KW, DC), lambda bi, t: (0, 0)),
            pl.BlockSpec((1, DC), lambda bi, t: (0, 0)),
            pl.BlockSpec((1, DC), lambda bi, t: (0, 0)),
            pl.BlockSpec((1, DC), lambda bi, t: (0, 0)),
        ],
        out_specs=[
            pl.BlockSpec((None, TT, DC), lambda bi, t: (bi, t, 0)),
            pl.BlockSpec((None, KW - 1, DC), lambda bi, t: (bi, 0, 0)),
        ],
        out_shape=[jax.ShapeDtypeStruct((B, L, DC), BF16), jax.ShapeDtypeStruct((B, KW - 1, DC), F32)],
        scratch_shapes=[pltpu.VMEM((HP + TT, DC), F32), pltpu.VMEM((TT, DC), F32)],
        compiler_params=_params("parallel", "arbitrary"),
        name="conv_branch",
    )(u, hist, conv_w, conv_b, g, b)


def _mlstm_kernel(q_ref, k_ref, v_ref, o_ref, gc_ref, gr_ref, c0_ref, n0_ref, m0_ref, mg_ref,
                  h_ref, cN_ref, nN_ref, mN_ref, c_scr, n_scr, m_scr, *, NH, LC):
    ck = pl.program_id(1)
    dh = c_scr.shape[-1]
    scale = dh ** -0.5

    @pl.when(ck == 0)
    def _():
        c_scr[...] = c0_ref[...]
        n_scr[...] = n0_ref[...]
        m_scr[...] = m0_ref[...]

    gcol = gc_ref[...]
    lane = lax.broadcasted_iota(jnp.int32, gcol.shape, 1)
    grow = gr_ref[...]
    ti = lax.broadcasted_iota(jnp.int32, (LC, LC), 0)
    si = lax.broadcasted_iota(jnp.int32, (LC, LC), 1)
    causal = si <= ti
    anti = ti <= si

    for hd in range(NH):
        cols = slice(hd * dh, (hd + 1) * dh)
        li_col = jnp.sum(jnp.where(lane == hd, gcol, 0.0), axis=1, keepdims=True)
        lf_col = _log_sigmoid(jnp.sum(jnp.where(lane == NH + hd, gcol, 0.0), axis=1, keepdims=True))
        li_row = grow[hd:hd + 1, :]
        lf_row = _log_sigmoid(grow[NH + hd:NH + hd + 1, :])

        f_col = jnp.sum(jnp.where(causal, lf_row, 0.0), axis=1, keepdims=True)
        f_row = jnp.sum(jnp.where(anti, lf_col, 0.0), axis=0, keepdims=True)
        g_row = li_row - f_row
        g_col = li_col - f_col

        m_prev = m_scr[hd, :, 0:1]
        a = jnp.where(causal, g_row, -jnp.inf)
        m_col = jnp.maximum(m_prev, jnp.max(a, axis=1, keepdims=True))
        dmat = jnp.exp(a - m_col)
        inter = jnp.exp(m_prev - m_col)

        q = q_ref[:, cols]
        k = k_ref[:, cols]
        v = v_ref[:, cols]
        c_prev = c_scr[hd]
        n_prev = n_scr[hd]
        s = lax.dot_general(q, k, (((1,), (1,)), ((), ())), preferred_element_type=F32) * scale * dmat
        num = (jnp.dot(s.astype(BF16), v, preferred_element_type=F32)
               + inter * jnp.dot(q, c_prev.astype(BF16), preferred_element_type=F32))
        den = (jnp.sum(s, axis=1, keepdims=True)
               + inter * jnp.sum(q.astype(F32) * n_prev, axis=1, keepdims=True))
        hcore = num / jnp.maximum(jnp.abs(den), jnp.exp(-(f_col + m_col)))

        mu = jnp.mean(hcore, axis=1, keepdims=True)
        d = hcore - mu
        var = jnp.mean(d * d, axis=1, keepdims=True)
        hn = d * lax.rsqrt(var + LN_EPS)
        gate = jax.nn.sigmoid(o_ref[:, cols].astype(F32))
        h_ref[:, cols] = (gate * hn * mg_ref[:, cols]).astype(h_ref.dtype)

        m_last = m_col[LC - 1:LC, :]
        f_tot = f_col[LC - 1:LC, :]
        decay = jnp.exp(m_prev - m_last)
        kw = k.astype(F32) * (jnp.exp(g_col - m_last) * scale)
        c_new = decay * c_prev + lax.dot_general(kw.astype(BF16), v, (((0,), (0,)), ((), ())),
                                                 preferred_element_type=F32)
        n_new = decay * n_prev + jnp.sum(kw, axis=0, keepdims=True)
        m_new = jnp.broadcast_to(f_tot + m_last, (1, LANES))
        c_scr[hd] = c_new
        n_scr[hd] = n_new
        m_scr[hd] = m_new
        cN_ref[hd] = c_new
        nN_ref[hd] = n_new
        mN_ref[hd] = m_new


def _mlstm_branch(qkvo, gates, gates_t, c0, n0, m0, mln_g, B, L):
    NH, dh = c0.shape[1], c0.shape[2]
    DML = NH * dh
    LC = _tile(L, 256, LANES) if L % LANES == 0 else L
    NC = L // LC
    kern = functools.partial(_mlstm_kernel, NH=NH, LC=LC)
    qkv_spec = lambda off: pl.BlockSpec((LC, DML), lambda b, c: (b * NC + c, off))
    state4 = lambda r, w: pl.BlockSpec((None, NH, r, w), lambda b, c: (b, 0, 0, 0))
    return pl.pallas_call(
        kern,
        grid=(B, NC),
        in_specs=[
            qkv_spec(0), qkv_spec(1), qkv_spec(2), qkv_spec(3),
            pl.BlockSpec((LC, LANES), lambda b, c: (b * NC + c, 0)),
            pl.BlockSpec((None, 2 * NH, LC), lambda b, c: (b, 0, c)),
            state4(dh, dh), state4(1, dh), state4(1, LANES),
            pl.BlockSpec((1, DML), lambda b, c: (0, 0)),
        ],
        out_specs=[
            pl.BlockSpec((LC, DML), lambda b, c: (b * NC + c, 0)),
            state4(dh, dh), state4(1, dh), state4(1, LANES),
        ],
        out_shape=[
            jax.ShapeDtypeStruct((B * L, DML), BF16),
            jax.ShapeDtypeStruct((B, NH, dh, dh), F32),
            jax.ShapeDtypeStruct((B, NH, 1, dh), F32),
            jax.ShapeDtypeStruct((B, NH, 1, LANES), F32),
        ],
        scratch_shapes=[pltpu.VMEM((NH, dh, dh), F32), pltpu.VMEM((NH, 1, dh), F32),
                        pltpu.VMEM((NH, 1, LANES), F32)],
        compiler_params=_params("parallel", "arbitrary"),
        name="mlstm_branch",
    )(qkvo, qkvo, qkvo, qkvo, gates, gates_t, c0, n0, m0, mln_g)


def _outproj_kernel(c_ref, h_ref, x_ref, wc_ref, wh_ref, g_ref, b_ref, rw_ref, rb_ref, cnt0_ref,
                    x1_ref, xp_ref, info_ref, cnt_ref, run_scr, *, alpha, NE):
    i = pl.program_id(0)
    TM, D = x_ref.shape
    half = D // 2
    epg = NE // N_GROUPS
    neg = -jnp.inf

    @pl.when(i == 0)
    def _():
        run_scr[...] = cnt0_ref[...]

    SM = TM // ROUTE_SPLIT if TM % (ROUTE_SPLIT * 2 * SUBLANES) == 0 else TM
    lane = lax.broadcasted_iota(jnp.int32, (SM, LANES), 1)
    lanef = lane.astype(F32)
    ri = lax.broadcasted_iota(jnp.int32, (SM, SM), 0)
    ci = lax.broadcasted_iota(jnp.int32, (SM, SM), 1)
    tri = jnp.where(ci < ri, 1.0, 0.0).astype(BF16)
    run = run_scr[...]

    for r0_ in range(0, TM, SM):
        rs = slice(r0_, r0_ + SM)
        mix = (jnp.dot(c_ref[rs, :], wc_ref[...], preferred_element_type=F32)
               + jnp.dot(h_ref[rs, :], wh_ref[...], preferred_element_type=F32))
        x1 = _ln_rows(alpha * x_ref[rs, :] + mix, g_ref[...], b_ref[...])
        x1_ref[rs, :] = x1

        x1h = x1.astype(BF16)
        x1hf = x1h.astype(F32)
        lo = lax.bitcast_convert_type(x1hf[:, :half], jnp.uint32)
        hi = lax.bitcast_convert_type(x1hf[:, half:], jnp.uint32)
        _store_tiled_rows(xp_ref, (hi & jnp.uint32(0xFFFF0000)) | (lo >> 16), half // LANES, r0_)

        x1l = (x1 - x1hf).astype(BF16)
        p = jnp.dot(x1h, rw_ref[...], preferred_element_type=F32)
        logits = (p[:, :LANES] + p[:, LANES:]
                  + jnp.dot(x1l, rw_ref[:, :LANES], preferred_element_type=F32))
        aff = jax.nn.sigmoid(logits)
        biased = jnp.where(lane < NE, aff + rb_ref[...], neg)
        best = e0 = e1 = None
        for grp in range(N_GROUPS):
            vg = jnp.where((lane >= grp * epg) & (lane < (grp + 1) * epg), biased, neg)
            m1 = jnp.max(vg, axis=1, keepdims=True)
            i1 = jnp.min(jnp.where(vg == m1, lanef, float(LANES)), axis=1, keepdims=True)
            vg2 = jnp.where(lanef == i1, neg, vg)
            m2 = jnp.max(vg2, axis=1, keepdims=True)
            i2 = jnp.min(jnp.where(vg2 == m2, lanef, float(LANES)), axis=1, keepdims=True)
            score = m1 + m2
            if grp == 0:
                best, e0, e1 = score, i1, i2
            else:
                better = score > best
                best = jnp.where(better, score, best)
                e0 = jnp.where(better, i1, e0)
                e1 = jnp.where(better, i2, e1)

        oh0 = lanef == e0
        oh1 = lanef == e1
        a0 = jnp.sum(jnp.where(oh0, aff, 0.0), axis=1, keepdims=True)
        a1 = jnp.sum(jnp.where(oh1, aff, 0.0), axis=1, keepdims=True)
        asum = a0 + a1
        onehot = jnp.where(oh0 | oh1, 1.0, 0.0)
        prefix = jnp.dot(tri, onehot.astype(BF16), preferred_element_type=F32) + run
        rk0 = jnp.sum(jnp.where(oh0, prefix, 0.0), axis=1, keepdims=True)
        rk1 = jnp.sum(jnp.where(oh1, prefix, 0.0), axis=1, keepdims=True)
        run = run + jnp.sum(onehot, axis=0, keepdims=True)

        cols = (e0, e1, rk0, rk1, a0 / asum, a1 / asum)
        info = jnp.zeros((SM, LANES), F32)
        for ci_, val in enumerate(cols):
            info = jnp.where(lane == ci_, val, info)
        info_ref[rs, :] = info

    run_scr[...] = run
    cnt_ref[...] = run


def _outproj_router(c, h, x, wc, wh, g, b, rw, rb, cnt0, alpha, NE):
    T, D = x.shape
    DC, DML = c.shape[1], h.shape[1]
    TM = _tile(T, 512)
    nch = D // 2 // LANES
    kern = functools.partial(_outproj_kernel, alpha=alpha, NE=NE)
    rowspec = lambda w: pl.BlockSpec((TM, w), lambda i: (i, 0))
    full = lambda r, w: pl.BlockSpec((r, w), lambda i: (0, 0))
    return pl.pallas_call(
        kern,
        grid=(T // TM,),
        in_specs=[rowspec(DC), rowspec(DML), rowspec(D), full(DC, D), full(DML, D), full(1, D), full(1, D),
                  full(D, 2 * LANES), full(1, LANES), full(1, LANES)],
        out_specs=[rowspec(D), pl.BlockSpec((TM * nch, LANES), lambda i: (i, 0)), rowspec(LANES),
                   full(1, LANES)],
        out_shape=[
            jax.ShapeDtypeStruct((T, D), F32),
            jax.ShapeDtypeStruct((T * nch, LANES), jnp.uint32),
            jax.ShapeDtypeStruct((T, LANES), F32),
            jax.ShapeDtypeStruct((1, LANES), F32),
        ],
        scratch_shapes=[pltpu.VMEM((1, LANES), F32)],
        compiler_params=_params("arbitrary"),
        name="outproj_router",
    )(c, h, x, wc, wh, g, b, rw, rb, cnt0)


def _scatter_kernel(dest_ref, pend_ref, x_ref, *rest, TMS, RB, NE, first):
    if first:
        xb_out, zeros, sem = rest

        @pl.when(pl.program_id(0) == 0)
        def _():
            zeros[...] = jnp.zeros(zeros.shape, zeros.dtype)

            def pad_copy(e):
                start = pl.multiple_of(pend_ref[e] - RB, RB)
                return pltpu.make_async_copy(zeros, xb_out.at[pl.ds(start, RB)], sem)

            def has_rows(e):
                return pend_ref[e] > (pend_ref[e - 1] if e > 0 else 0)

            for e in range(NE):
                @pl.when(has_rows(e))
                def _():
                    pad_copy(e).start()

            for e in range(NE):
                @pl.when(has_rows(e))
                def _():
                    pad_copy(e).wait()
    else:
        _, xb_out, sem = rest

    nch = xb_out.shape[1]

    def copy(r, kk):
        return pltpu.make_async_copy(x_ref.at[pl.ds(pl.multiple_of(r * nch, nch), nch)],
                                     xb_out.at[dest_ref[TOP_K * r + kk]], sem)

    def issue(r, carry):
        for kk in range(TOP_K):
            copy(r, kk).start(priority=kk % 2)
        return carry

    lax.fori_loop(0, TMS, issue, 0, unroll=DMA_UNROLL)

    def drain(r, carry):
        for kk in range(TOP_K):
            copy(r, kk).wait()
        return carry

    lax.fori_loop(0, TMS, drain, 0, unroll=DMA_UNROLL)


def _scatter_rows(dest_flat, pend, xp, xb, P, RB):
    T = dest_flat.shape[0] // TOP_K
    nch = xp.shape[0] // T
    NE = pend.shape[0]
    TMS = _tile(T, 512)
    first = xb is None
    kern = functools.partial(_scatter_kernel, TMS=TMS, RB=RB, NE=NE, first=first)
    in_specs = [
        pl.BlockSpec((TOP_K * TMS,), lambda i: (i,), memory_space=pltpu.SMEM),
        pl.BlockSpec((NE,), lambda i: (0,), memory_space=pltpu.SMEM),
        pl.BlockSpec((TMS * nch, LANES), lambda i: (i, 0)),
    ]
    scratch = [pltpu.SemaphoreType.DMA(())]
    args = [dest_flat, pend, xp]
    if first:
        scratch = [pltpu.VMEM((RB, nch, LANES), xp.dtype)] + scratch
    else:
        in_specs.append(pl.BlockSpec(memory_space=pl.ANY))
        args.append(xb)
    return pl.pallas_call(
        kern,
        grid=(T // TMS,),
        in_specs=in_specs,
        out_specs=pl.BlockSpec(memory_space=pl.ANY),
        out_shape=jax.ShapeDtypeStruct((P, nch, LANES), xp.dtype),
        scratch_shapes=scratch,
        input_output_aliases={} if first else {3: 0},
        compiler_params=_params("arbitrary"),
        name="moe_scatter",
    )(*args)


def _new_expert(i, blk_e):
    return (i == 0) | (blk_e[i] != blk_e[jnp.maximum(i - 1, 0)])


def _pack_halves(x):
    h = x.shape[-1] // 2
    lo = lax.bitcast_convert_type(x[:, :h].astype(BF16).astype(F32), jnp.uint32)
    hi = lax.bitcast_convert_type(x[:, h:].astype(BF16).astype(F32), jnp.uint32)
    return (hi & jnp.uint32(0xFFFF0000)) | (lo >> 16)


def _unpack_halves(w):
    lo = lax.bitcast_convert_type(w << 16, F32)
    hi = lax.bitcast_convert_type(w & jnp.uint32(0xFFFF0000), F32)
    return lo, hi


def _expert_kernel(blk_e, blk_x, nxt_e, n_used, xb_ref, wg_hbm, wu_hbm, wd_hbm, y_ref,
                   stg, stu, std, wgs, wus, wds, sems, *, layer):
    del blk_x
    i = pl.program_id(0)
    half = wgs.shape[0] // 2
    nch = half // LANES
    RB = xb_ref.shape[0] // nch

    def weight_copies(e):
        return (pltpu.make_async_copy(wg_hbm.at[layer, e], stg, sems.at[0]),
                pltpu.make_async_copy(wu_hbm.at[layer, e], stu, sems.at[1]),
                pltpu.make_async_copy(wd_hbm.at[layer, e], std, sems.at[2]))

    @pl.when(i < n_used[0])
    def _():
        @pl.when(_new_expert(i, blk_e))
        def _():
            @pl.when(i == 0)
            def _():
                for cp in weight_copies(blk_e[i]):
                    cp.start()

            for cp in weight_copies(blk_e[i]):
                cp.wait()
            wgs[...] = stg[...].astype(BF16)
            wus[...] = stu[...].astype(BF16)
            wds[...] = std[...].astype(BF16)

            @pl.when(nxt_e[i] >= 0)
            def _():
                for cp in weight_copies(nxt_e[i]):
                    cp.start()

        lo, hi = _unpack_halves(_load_tiled_rows(xb_ref, RB, nch))
        lo = lo.astype(BF16)
        hi = hi.astype(BF16)
        g = (jnp.dot(lo, wgs[0:half, :], preferred_element_type=F32)
             + jnp.dot(hi, wgs[half:2 * half, :], preferred_element_type=F32))
        u = (jnp.dot(lo, wus[0:half, :], preferred_element_type=F32)
             + jnp.dot(hi, wus[half:2 * half, :], preferred_element_type=F32))
        hid = (g * jax.nn.sigmoid(g) * u).astype(BF16)
        _store_tiled_rows(y_ref, _pack_halves(jnp.dot(hid, wds[...], preferred_element_type=F32)), nch)


def _expert_ffn(layer, blk_e, blk_x, nxt_e, n_used, xb, w_gate, w_up, w_down, RB):
    D, DFF = w_gate.shape[-2:]
    half = D // 2
    nch = half // LANES
    P = xb.shape[0] // nch
    rows = pl.BlockSpec((RB * nch, LANES), lambda i, be, bx, nx, nu: (bx[i], 0))
    anyspec = pl.BlockSpec(memory_space=pl.ANY)
    return pl.pallas_call(
        functools.partial(_expert_kernel, layer=layer),
        grid_spec=pltpu.PrefetchScalarGridSpec(
            num_scalar_prefetch=4,
            grid=(P // RB,),
            in_specs=[rows, anyspec, anyspec, anyspec],
            out_specs=rows,
            scratch_shapes=[pltpu.VMEM((D, DFF), F32), pltpu.VMEM((D, DFF), F32), pltpu.VMEM((DFF, D), F32),
                            pltpu.VMEM((D, DFF), BF16), pltpu.VMEM((D, DFF), BF16), pltpu.VMEM((DFF, D), BF16),
                            pltpu.SemaphoreType.DMA((3,))],
        ),
        out_shape=jax.ShapeDtypeStruct((P * nch, LANES), jnp.uint32),
        compiler_params=_params("arbitrary"),
        name="expert_ffn",
    )(blk_e, blk_x, nxt_e, n_used, xb, w_gate, w_up, w_down)


def _combine_kernel(dest_ref, x1_ref, info_ref, g_ref, b_ref, yb_hbm, x2_ref, x2b_ref, buf, sem,
                    *, alpha, TMC):
    nch = yb_hbm.shape[1]

    def copy(r, kk):
        return pltpu.make_async_copy(yb_hbm.at[dest_ref[TOP_K * r + kk]],
                                     buf.at[kk, pl.ds(pl.multiple_of(r * nch, nch), nch)], sem)

    def issue(r, carry):
        for kk in range(TOP_K):
            copy(r, kk).start(priority=kk % 2)
        return carry

    lax.fori_loop(0, TMC, issue, 0, unroll=DMA_UNROLL)

    def drain(r, carry):
        for kk in range(TOP_K):
            copy(r, kk).wait()
        return carry

    lax.fori_loop(0, TMC, drain, 0, unroll=DMA_UNROLL)

    info = info_ref[...]
    g0 = info[:, 4:5]
    g1 = info[:, 5:6]
    half = nch * LANES

    def rows_of(kk):
        w = jnp.concatenate([buf[kk, pl.ds(c, TMC, stride=nch), :] for c in range(nch)], axis=1)
        return _unpack_halves(w)

    lo0, hi0 = rows_of(0)
    lo1, hi1 = rows_of(1)
    zl = alpha * x1_ref[:, 0:half] + (g0 * lo0 + g1 * lo1)
    zh = alpha * x1_ref[:, half:2 * half] + (g0 * hi0 + g1 * hi1)
    inv_d = 1.0 / (2 * half)
    mu = (jnp.sum(zl, axis=-1, keepdims=True) + jnp.sum(zh, axis=-1, keepdims=True)) * inv_d
    dl = zl - mu
    dh_ = zh - mu
    var = (jnp.sum(dl * dl, axis=-1, keepdims=True) + jnp.sum(dh_ * dh_, axis=-1, keepdims=True)) * inv_d
    rs = lax.rsqrt(var + LN_EPS)
    xl = dl * rs * g_ref[:, 0:half] + b_ref[:, 0:half]
    xh = dh_ * rs * g_ref[:, half:2 * half] + b_ref[:, half:2 * half]
    x2_ref[:, 0:half] = xl
    x2_ref[:, half:2 * half] = xh
    x2b_ref[:, 0:half] = xl.astype(BF16)
    x2b_ref[:, half:2 * half] = xh.astype(BF16)


def _combine_ln(dest_flat, x1, info, g, b, yb, alpha):
    T, D = x1.shape
    TMC = _tile(T, 512)
    kern = functools.partial(_combine_kernel, alpha=alpha, TMC=TMC)
    rowspec = lambda w: pl.BlockSpec((TMC, w), lambda i: (i, 0))
    return pl.pallas_call(
        kern,
        grid=(T // TMC,),
        in_specs=[
            pl.BlockSpec((TOP_K * TMC,), lambda i: (i,), memory_space=pltpu.SMEM),
            rowspec(D), rowspec(LANES),
            pl.BlockSpec((1, D), lambda i: (0, 0)), pl.BlockSpec((1, D), lambda i: (0, 0)),
            pl.BlockSpec(memory_space=pl.ANY),
        ],
        out_specs=[rowspec(D), rowspec(D)],
        out_shape=[jax.ShapeDtypeStruct((T, D), F32), jax.ShapeDtypeStruct((T, D), BF16)],
        scratch_shapes=[pltpu.VMEM((TOP_K, TMC * (D // 2 // LANES), LANES), jnp.uint32),
                        pltpu.SemaphoreType.DMA(())],
        compiler_params=_params("arbitrary"),
        name="moe_combine",
    )(dest_flat, x1, info, g, b, yb)


def _dispatch_plan(cnt, infos, NE, RB, NBLK):
    counts = cnt[0, :NE].astype(jnp.int32)
    pcounts = (counts + RB - 1) // RB * RB
    pend = jnp.cumsum(pcounts)
    pstart = pend - pcounts
    dests = []
    for info in infos:
        e = info[:, 0:TOP_K].astype(jnp.int32)
        rank = info[:, TOP_K:2 * TOP_K].astype(jnp.int32)
        onehot = (e[..., None] == jnp.arange(NE, dtype=jnp.int32)).astype(jnp.int32)
        dests.append((jnp.sum(onehot * pstart, axis=-1) + rank).reshape(-1))
    n_used = jnp.maximum(pend[-1] // RB, 1).astype(jnp.int32)
    blk_x = jnp.minimum(jnp.arange(NBLK, dtype=jnp.int32), n_used - 1)
    blk_e = jnp.sum((pend[None, :] <= (blk_x * RB)[:, None]).astype(jnp.int32), axis=1)
    blk_e = jnp.minimum(blk_e, NE - 1)
    ids = jnp.arange(NE, dtype=jnp.int32)
    later = (ids[None, :] > ids[:, None]) & (counts > 0)[None, :]
    nxt = jnp.min(jnp.where(later, ids[None, :], NE), axis=1)
    nxt = jnp.where(nxt == NE, -1, nxt)
    nxt_e = jnp.sum((blk_e[:, None] == ids[None, :]).astype(jnp.int32) * nxt[None, :], axis=1)
    return dests, pend.astype(jnp.int32), blk_e, blk_x, nxt_e.astype(jnp.int32), n_used.reshape(1)


def kernel(x_prompt, x_sample, state_conv, state_C, state_n, state_m, w_in, b_in, conv_w, conv_b,
           conv_ln_g, conv_ln_b, ml_norm_g, w_out, ln1_g, ln1_b, router_w, router_bias,
           w_gate, w_up, w_down, ln2_g, ln2_b):
    depth, D, _ = w_in.shape
    DC = conv_w.shape[-1]
    NH, dh = state_C.shape[2], state_C.shape[3]
    DML = NH * dh
    NE = router_w.shape[1]
    alpha = (2.0 * depth) ** 0.25
    KW = conv_w.shape[1]

    streams = []
    Bp, Lp, _ = x_prompt.shape
    Bs, Ls, _ = x_sample.shape
    streams.append(dict(B=Bp, L=Lp, x=x_prompt.reshape(Bp * Lp, D), prompt=True))
    streams.append(dict(B=Bs, L=Ls, x=x_sample.reshape(Bs * Ls, D), prompt=False))
    for s in streams:
        s['xb'] = s['x']
        s['outs'] = dict(conv=[], C=[], n=[], m=[])

    A = TOP_K * (Bp * Lp + Bs * Ls)
    RB = ROW_BLOCK if A >= 32 * ROW_BLOCK else 32
    NBLK = -(-A // RB) + NE
    P = NBLK * RB

    rw_f = jnp.zeros((D, LANES), F32).at[:, :NE].set(router_w.astype(F32))
    rw_hi = rw_f.astype(BF16)
    rw = jnp.concatenate([rw_hi, (rw_f - rw_hi.astype(F32)).astype(BF16)], axis=1)
    rb = jnp.zeros((1, LANES), F32).at[0, :NE].set(router_bias.astype(F32))
    row2 = lambda v: v.reshape(1, -1)

    for l in range(depth):
        wl = w_in[l]
        wa = wl[:, :DC].astype(BF16)
        wg = wl[:, DC:2 * DC].astype(BF16)
        wqkvo = wl[:, 2 * DC:2 * DC + 4 * DML].astype(BF16)
        wgt = jnp.zeros((D, LANES), BF16).at[:, :2 * NH].set(wl[:, 2 * DC + 4 * DML:].astype(BF16))
        bl = b_in[l]
        bgt = jnp.zeros((1, LANES), F32).at[0, :2 * NH].set(bl[2 * DC + 4 * DML:])
        wc = w_out[l, :DC].astype(BF16)
        wh = w_out[l, DC:].astype(BF16)

        cnt = jnp.zeros((1, LANES), F32)
        for s in streams:
            B, L = s['B'], s['L']
            u, gates, *xb_new = _glu_proj(s['xb'], wa, wg, row2(bl[:DC]), row2(bl[DC:2 * DC]), wgt, bgt)
            if xb_new:
                s['xb'] = xb_new[0]
            qkvo = _mm_bias(s['xb'], wqkvo, row2(bl[2 * DC:2 * DC + 4 * DML]), BF16)
            if s['prompt']:
                hist = jnp.zeros((B, KW - 1, DC), F32)
                c0 = jnp.zeros((B, NH, dh, dh), F32)
                n0 = jnp.zeros((B, NH, 1, dh), F32)
                m0 = jnp.zeros((B, NH, 1, LANES), F32)
            else:
                hist = state_conv[l]
                c0 = state_C[l]
                n0 = state_n[l].reshape(B, NH, 1, dh)
                m0 = jnp.broadcast_to(state_m[l].reshape(B, NH, 1, 1), (B, NH, 1, LANES))
            c, new_hist = _conv_branch(u.reshape(B, L, DC), hist, conv_w[l], row2(conv_b[l]),
                                       row2(conv_ln_g[l]), row2(conv_ln_b[l]))
            gates_t = gates[:, :2 * NH].reshape(B, L, 2 * NH).transpose(0, 2, 1)
            hout, cN, nN, mN = _mlstm_branch(qkvo, gates, gates_t, c0, n0, m0, row2(ml_norm_g[l]), B, L)
            s['outs']['conv'].append(new_hist)
            s['outs']['C'].append(cN)
            s['outs']['n'].append(nN.reshape(B, NH, dh))
            s['outs']['m'].append(mN[:, :, 0, 0])
            x1, xp, info, cnt = _outproj_router(c.reshape(B * L, DC), hout, s['x'], wc, wh,
                                                row2(ln1_g[l]), row2(ln1_b[l]), rw, rb, cnt, alpha, NE)
            s['x1'], s['xp'], s['info'] = x1, xp, info

        dests, pend, blk_e, blk_x, nxt_e, n_used = _dispatch_plan(
            cnt, [s['info'] for s in streams], NE, RB, NBLK)
        xb = None
        for s, dest in zip(streams, dests):
            xb = _scatter_rows(dest, pend, s['xp'], xb, P, RB)
        nch = D // 2 // LANES
        yb = _expert_ffn(l, blk_e, blk_x, nxt_e, n_used, xb.reshape(P * nch, LANES), w_gate, w_up, w_down, RB)
        yb = yb.reshape(P, nch, LANES)
        for s, dest in zip(streams, dests):
            s['x'], s['xb'] = _combine_ln(dest, s['x1'], s['info'], row2(ln2_g[l]), row2(ln2_b[l]), yb, alpha)

    outs = []
    for s in streams:
        outs.append(s['x'].reshape(s['B'], s['L'], D))
    for s in streams:
        o = s['outs']
        outs += [jnp.stack(o['conv']), jnp.stack(o['C']), jnp.stack(o['n']), jnp.stack(o['m'])]
    return tuple(outs)
```

```python
import functools

import jax
import jax.numpy as jnp
from jax import lax
from jax.experimental import pallas as pl
from jax.experimental.pallas import tpu as pltpu

F32 = jnp.float32
BF16 = jnp.bfloat16

N_GROUPS = 4
TOP_K = 2
LN_EPS = 1e-5
LANES = 128
SUBLANES = 8
VMEM_LIMIT = 56 * 1024 * 1024
DMA_UNROLL = 8
ROW_BLOCK = 256
ROUTE_SPLIT = 1


def _tile(n, target, align=SUBLANES):
    t = min(n, target)
    t -= t % align
    while t >= align:
        if n % t == 0:
            return t
        t -= align
    return n


def _params(*sem):
    return pltpu.CompilerParams(dimension_semantics=sem, vmem_limit_bytes=VMEM_LIMIT)


def _ln_rows(y, g, b):
    mu = jnp.mean(y, axis=-1, keepdims=True)
    d = y - mu
    var = jnp.mean(d * d, axis=-1, keepdims=True)
    return d * lax.rsqrt(var + LN_EPS) * g + b


def _log_sigmoid(x):
    return jnp.minimum(x, 0.0) - jnp.log1p(jnp.exp(-jnp.abs(x)))


def _load_tiled_rows(ref, n, nch, row0=0):
    return jnp.concatenate([ref[pl.ds(row0 * nch + c, n, stride=nch), :] for c in range(nch)], axis=1)


def _store_tiled_rows(ref, val, nch, row0=0):
    n = val.shape[0]
    for c in range(nch):
        ref[pl.ds(row0 * nch + c, n, stride=nch), :] = val[:, c * LANES:(c + 1) * LANES]


def _glu_kernel(x_ref, wa_ref, wg_ref, ba_ref, bg_ref, wgt_ref, bgt_ref, u_ref, gates_ref, *xb_ref):
    if xb_ref:
        @pl.when(pl.program_id(1) == 0)
        def _():
            xb_ref[0][...] = x_ref[...].astype(BF16)

        x = xb_ref[0][...]
    else:
        x = x_ref[...]
    a = jnp.dot(x, wa_ref[...], preferred_element_type=F32) + ba_ref[...]
    g = jnp.dot(x, wg_ref[...], preferred_element_type=F32) + bg_ref[...]
    u_ref[...] = a * jax.nn.sigmoid(g)

    @pl.when(pl.program_id(1) == 0)
    def _():
        gates_ref[...] = jnp.dot(x, wgt_ref[...], preferred_element_type=F32) + bgt_ref[...]


def _glu_proj(xb, w_in, b_in, layer, DC, wgt, bgt):
    T, D = xb.shape
    TM = _tile(T, 1024)
    TN = _tile(DC, 512, LANES)
    goff = DC // TN
    out_specs = [
        pl.BlockSpec((TM, TN), lambda m, n: (m, n)),
        pl.BlockSpec((TM, LANES), lambda m, n: (m, 0)),
    ]
    out_shape = [jax.ShapeDtypeStruct((T, DC), F32), jax.ShapeDtypeStruct((T, LANES), F32)]
    if xb.dtype != BF16:
        out_specs.append(pl.BlockSpec((TM, D), lambda m, n: (m, 0)))
        out_shape.append(jax.ShapeDtypeStruct((T, D), BF16))
    return pl.pallas_call(
        _glu_kernel,
        grid=(T // TM, DC // TN),
        in_specs=[
            pl.BlockSpec((TM, D), lambda m, n: (m, 0)),
            pl.BlockSpec((None, D, TN), lambda m, n: (layer, 0, n)),
            pl.BlockSpec((None, D, TN), lambda m, n: (layer, 0, n + goff)),
            pl.BlockSpec((None, 1, TN), lambda m, n: (layer, 0, n)),
            pl.BlockSpec((None, 1, TN), lambda m, n: (layer, 0, n + goff)),
            pl.BlockSpec((D, LANES), lambda m, n: (0, 0)),
            pl.BlockSpec((1, LANES), lambda m, n: (0, 0)),
        ],
        out_specs=out_specs,
        out_shape=out_shape,
        compiler_params=_params("parallel", "arbitrary"),
        name="glu_proj",
    )(xb, w_in, w_in, b_in, b_in, wgt, bgt)


def _mm_kernel(x_ref, w_ref, b_ref, o_ref):
    acc = jnp.dot(x_ref[...], w_ref[...], preferred_element_type=F32) + b_ref[...]
    o_ref[...] = acc.astype(o_ref.dtype)


def _mm_bias(xb, w_in, b_in, layer, col0, N, out_dtype):
    T, D = xb.shape
    TM = _tile(T, 1024)
    TN = next(t for t in range(min(N, 1024), 0, -LANES) if N % t == 0 and col0 % t == 0)
    noff = col0 // TN
    return pl.pallas_call(
        _mm_kernel,
        grid=(T // TM, N // TN),
        in_specs=[
            pl.BlockSpec((TM, D), lambda m, n: (m, 0)),
            pl.BlockSpec((None, D, TN), lambda m, n: (layer, 0, n + noff)),
            pl.BlockSpec((None, 1, TN), lambda m, n: (layer, 0, n + noff)),
        ],
        out_specs=pl.BlockSpec((TM, TN), lambda m, n: (m, n)),
        out_shape=jax.ShapeDtypeStruct((T, N), out_dtype),
        compiler_params=_params("parallel", "parallel"),
        name="qkvo_proj",
    )(xb, w_in, b_in)


def _conv_kernel(u_ref, hist_ref, w_ref, cb_ref, g_ref, b_ref, c_ref, nh_ref, ext, cbuf,
                 *, TT, KW, HP, RC, CC):
    t = pl.program_id(1)
    H = KW - 1
    DC = u_ref.shape[-1]

    @pl.when(t == 0)
    def _():
        ext[HP - H:HP, :] = hist_ref[...]

    ext[HP:HP + TT, :] = u_ref[...]

    WIN = RC + HP
    for r0 in range(0, TT, RC):
        for c0 in range(0, DC, CC):
            win = ext[r0:r0 + WIN, c0:c0 + CC]
            acc = None
            for sh in range(SUBLANES):
                shifted = win if sh == 0 else pltpu.roll(win, WIN - sh, axis=0)
                for s in range(sh, HP + 1, SUBLANES):
                    j = s - (HP - H)
                    if 0 <= j < KW:
                        term = w_ref[j:j + 1, c0:c0 + CC] * shifted[s - sh:s - sh + RC]
                        acc = term if acc is None else acc + term
            cbuf[r0:r0 + RC, c0:c0 + CC] = acc + cb_ref[:, c0:c0 + CC]

    y = _ln_rows(cbuf[...], g_ref[...], b_ref[...])
    c_ref[...] = (y * jax.nn.sigmoid(y)).astype(c_ref.dtype)

    tail = ext[HP + TT - H:HP + TT, :]
    ext[HP - H:HP, :] = tail

    @pl.when(t == pl.num_programs(1) - 1)
    def _():
        nh_ref[...] = tail


def _conv_branch(u, hist, conv_w, conv_b, g, b):
    B, L, DC = u.shape
    KW = conv_w.shape[0]
    TT = _tile(L, 512)
    assert TT >= KW - 1 and TT % SUBLANES == 0
    HP = -(-(KW - 1) // SUBLANES) * SUBLANES
    RC = _tile(TT, 128)
    CC = LANES
    kern = functools.partial(_conv_kernel, TT=TT, KW=KW, HP=HP, RC=RC, CC=CC)
    return pl.pallas_call(
        kern,
        grid=(B, L // TT),
        in_specs=[
            pl.BlockSpec((None, TT, DC), lambda bi, t: (bi, t, 0)),
            pl.BlockSpec((None, KW - 1, DC), lambda bi, t: (bi, 0, 0)),
            pl.BlockSpec((KW, DC), lambda bi, t: (0, 0)),
            pl.BlockSpec((1, DC), lambda bi, t: (0, 0)),
            pl.BlockSpec((1, DC), lambda bi, t: (0, 0)),
            pl.BlockSpec((1, DC), lambda bi, t: (0, 0)),
        ],
        out_specs=[
            pl.BlockSpec((None, TT, DC), lambda bi, t: (bi, t, 0)),
            pl.BlockSpec((None, KW - 1, DC), lambda bi, t: (bi, 0, 0)),
        ],
        out_shape=[jax.ShapeDtypeStruct((B, L, DC), BF16), jax.ShapeDtypeStruct((B, KW - 1, DC), F32)],
        scratch_shapes=[pltpu.VMEM((HP + TT, DC), F32), pltpu.VMEM((TT, DC), F32)],
        compiler_params=_params("parallel", "arbitrary"),
        name="conv_branch",
    )(u, hist, conv_w, conv_b, g, b)


def _mlstm_kernel(q_ref, k_ref, v_ref, o_ref, gc_ref, gr_ref, c0_ref, n0_ref, m0_ref, mg_ref,
                  h_ref, cN_ref, nN_ref, mN_ref, c_scr, n_scr, m_scr, *, NH, LC):
    ck = pl.program_id(1)
    dh = c_scr.shape[-1]
    scale = dh ** -0.5

    @pl.when(ck == 0)
    def _():
        c_scr[...] = c0_ref[...]
        n_scr[...] = n0_ref[...]
        m_scr[...] = m0_ref[...]

    gcol = gc_ref[...]
    lane = lax.broadcasted_iota(jnp.int32, gcol.shape, 1)
    grow = gr_ref[...]
    ti = lax.broadcasted_iota(jnp.int32, (LC, LC), 0)
    si = lax.broadcasted_iota(jnp.int32, (LC, LC), 1)
    causal = si <= ti
    anti = ti <= si

    for hd in range(NH):
        cols = slice(hd * dh, (hd + 1) * dh)
        li_col = jnp.sum(jnp.where(lane == hd, gcol, 0.0), axis=1, keepdims=True)
        lf_col = _log_sigmoid(jnp.sum(jnp.where(lane == NH + hd, gcol, 0.0), axis=1, keepdims=True))
        li_row = grow[hd:hd + 1, :]
        lf_row = _log_sigmoid(grow[NH + hd:NH + hd + 1, :])

        f_col = jnp.sum(jnp.where(causal, lf_row, 0.0), axis=1, keepdims=True)
        f_row = jnp.sum(jnp.where(anti, lf_col, 0.0), axis=0, keepdims=True)
        g_row = li_row - f_row
        g_col = li_col - f_col

        m_prev = m_scr[hd, :, 0:1]
        a = jnp.where(causal, g_row, -jnp.inf)
        m_col = jnp.maximum(m_prev, jnp.max(a, axis=1, keepdims=True))
        dmat = jnp.exp(a - m_col)
        inter = jnp.exp(m_prev - m_col)

        q = q_ref[:, cols]
        k = k_ref[:, cols]
        v = v_ref[:, cols]
        c_prev = c_scr[hd]
        n_prev = n_scr[hd]
        s = lax.dot_general(q, k, (((1,), (1,)), ((), ())), preferred_element_type=F32) * scale * dmat
        num = (jnp.dot(s.astype(BF16), v, preferred_element_type=F32)
               + inter * jnp.dot(q, c_prev.astype(BF16), preferred_element_type=F32))
        den = (jnp.sum(s, axis=1, keepdims=True)
               + inter * jnp.sum(q.astype(F32) * n_prev, axis=1, keepdims=True))
        hcore = num / jnp.maximum(jnp.abs(den), jnp.exp(-(f_col + m_col)))

        mu = jnp.mean(hcore, axis=1, keepdims=True)
        d = hcore - mu
        var = jnp.mean(d * d, axis=1, keepdims=True)
        hn = d * lax.rsqrt(var + LN_EPS)
        gate = jax.nn.sigmoid(o_ref[:, cols].astype(F32))
        h_ref[:, cols] = (gate * hn * mg_ref[:, cols]).astype(h_ref.dtype)

        m_last = m_col[LC - 1:LC, :]
        f_tot = f_col[LC - 1:LC, :]
        decay = jnp.exp(m_prev - m_last)
        kw = k.astype(F32) * (jnp.exp(g_col - m_last) * scale)
        c_new = decay * c_prev + lax.dot_general(kw.astype(BF16), v, (((0,), (0,)), ((), ())),
                                                 preferred_element_type=F32)
        n_new = decay * n_prev + jnp.sum(kw, axis=0, keepdims=True)
        m_new = jnp.broadcast_to(f_tot + m_last, (1, LANES))
        c_scr[hd] = c_new
        n_scr[hd] = n_new
        m_scr[hd] = m_new
        cN_ref[hd] = c_new
        nN_ref[hd] = n_new
        mN_ref[hd] = m_new


def _mlstm_branch(qkvo, gates, gates_t, c0, n0, m0, mln_g, B, L):
    NH, dh = c0.shape[1], c0.shape[2]
    DML = NH * dh
    LC = _tile(L, 256, LANES) if L % LANES == 0 else L
    NC = L // LC
    kern = functools.partial(_mlstm_kernel, NH=NH, LC=LC)
    qkv_spec = lambda off: pl.BlockSpec((LC, DML), lambda b, c: (b * NC + c, off))
    state4 = lambda r, w: pl.BlockSpec((None, NH, r, w), lambda b, c: (b, 0, 0, 0))
    return pl.pallas_call(
        kern,
        grid=(B, NC),
        in_specs=[
            qkv_spec(0), qkv_spec(1), qkv_spec(2), qkv_spec(3),
            pl.BlockSpec((LC, LANES), lambda b, c: (b * NC + c, 0)),
            pl.BlockSpec((None, 2 * NH, LC), lambda b, c: (b, 0, c)),
            state4(dh, dh), state4(1, dh), state4(1, LANES),
            pl.BlockSpec((1, DML), lambda b, c: (0, 0)),
        ],
        out_specs=[
            pl.BlockSpec((LC, DML), lambda b, c: (b * NC + c, 0)),
            state4(dh, dh), state4(1, dh), state4(1, LANES),
        ],
        out_shape=[
            jax.ShapeDtypeStruct((B * L, DML), BF16),
            jax.ShapeDtypeStruct((B, NH, dh, dh), F32),
            jax.ShapeDtypeStruct((B, NH, 1, dh), F32),
            jax.ShapeDtypeStruct((B, NH, 1, LANES), F32),
        ],
        scratch_shapes=[pltpu.VMEM((NH, dh, dh), F32), pltpu.VMEM((NH, 1, dh), F32),
                        pltpu.VMEM((NH, 1, LANES), F32)],
        compiler_params=_params("parallel", "arbitrary"),
        name="mlstm_branch",
    )(qkvo, qkvo, qkvo, qkvo, gates, gates_t, c0, n0, m0, mln_g)


def _outproj_kernel(c_ref, h_ref, x_ref, wc_ref, wh_ref, g_ref, b_ref, rw_ref, rb_ref, cnt0_ref,
                    x1_ref, xp_ref, info_ref, cnt_ref, run_scr, *, alpha, NE):
    i = pl.program_id(0)
    TM, D = x_ref.shape
    half = D // 2
    epg = NE // N_GROUPS
    neg = -jnp.inf

    @pl.when(i == 0)
    def _():
        run_scr[...] = cnt0_ref[...]

    SM = TM // ROUTE_SPLIT if TM % (ROUTE_SPLIT * 2 * SUBLANES) == 0 else TM
    lane = lax.broadcasted_iota(jnp.int32, (SM, LANES), 1)
    lanef = lane.astype(F32)
    ri = lax.broadcasted_iota(jnp.int32, (SM, SM), 0)
    ci = lax.broadcasted_iota(jnp.int32, (SM, SM), 1)
    tri = jnp.where(ci < ri, 1.0, 0.0).astype(BF16)
    run = run_scr[...]

    for r0_ in range(0, TM, SM):
        rs = slice(r0_, r0_ + SM)
        mix = (jnp.dot(c_ref[rs, :], wc_ref[...], preferred_element_type=F32)
               + jnp.dot(h_ref[rs, :], wh_ref[...], preferred_element_type=F32))
        x1 = _ln_rows(alpha * x_ref[rs, :] + mix, g_ref[...], b_ref[...])
        x1_ref[rs, :] = x1

        x1h = x1.astype(BF16)
        x1hf = x1h.astype(F32)
        lo = lax.bitcast_convert_type(x1hf[:, :half], jnp.uint32)
        hi = lax.bitcast_convert_type(x1hf[:, half:], jnp.uint32)
        _store_tiled_rows(xp_ref, (hi & jnp.uint32(0xFFFF0000)) | (lo >> 16), half // LANES, r0_)

        x1l = (x1 - x1hf).astype(BF16)
        p = jnp.dot(x1h, rw_ref[...], preferred_element_type=F32)
        logits = (p[:, :LANES] + p[:, LANES:]
                  + jnp.dot(x1l, rw_ref[:, :LANES], preferred_element_type=F32))
        aff = jax.nn.sigmoid(logits)
        biased = jnp.where(lane < NE, aff + rb_ref[...], neg)
        best = e0 = e1 = None
        for grp in range(N_GROUPS):
            vg = jnp.where((lane >= grp * epg) & (lane < (grp + 1) * epg), biased, neg)
            m1 = jnp.max(vg, axis=1, keepdims=True)
            i1 = jnp.min(jnp.where(vg == m1, lanef, float(LANES)), axis=1, keepdims=True)
            vg2 = jnp.where(lanef == i1, neg, vg)
            m2 = jnp.max(vg2, axis=1, keepdims=True)
            i2 = jnp.min(jnp.where(vg2 == m2, lanef, float(LANES)), axis=1, keepdims=True)
            score = m1 + m2
            if grp == 0:
                best, e0, e1 = score, i1, i2
            else:
                better = score > best
                best = jnp.where(better, score, best)
                e0 = jnp.where(better, i1, e0)
                e1 = jnp.where(better, i2, e1)

        oh0 = lanef == e0
        oh1 = lanef == e1
        a0 = jnp.sum(jnp.where(oh0, aff, 0.0), axis=1, keepdims=True)
        a1 = jnp.sum(jnp.where(oh1, aff, 0.0), axis=1, keepdims=True)
        asum = a0 + a1
        onehot = jnp.where(oh0 | oh1, 1.0, 0.0)
        prefix = jnp.dot(tri, onehot.astype(BF16), preferred_element_type=F32) + run
        rk0 = jnp.sum(jnp.where(oh0, prefix, 0.0), axis=1, keepdims=True)
        rk1 = jnp.sum(jnp.where(oh1, prefix, 0.0), axis=1, keepdims=True)
        run = run + jnp.sum(onehot, axis=0, keepdims=True)

        cols = (e0, e1, rk0, rk1, a0 / asum, a1 / asum)
        info = jnp.zeros((SM, LANES), F32)
        for ci_, val in enumerate(cols):
            info = jnp.where(lane == ci_, val, info)
        info_ref[rs, :] = info

    run_scr[...] = run
    cnt_ref[...] = run


def _outproj_router(c, h, x, w_out, layer, g, b, rw, rb, cnt0, alpha, NE):
    T, D = x.shape
    DC, DML = c.shape[1], h.shape[1]
    assert DC % DML == 0
    TM = _tile(T, 512)
    nch = D // 2 // LANES
    kern = functools.partial(_outproj_kernel, alpha=alpha, NE=NE)
    rowspec = lambda w: pl.BlockSpec((TM, w), lambda i: (i, 0))
    full = lambda r, w: pl.BlockSpec((r, w), lambda i: (0, 0))
    wspec = lambda r, blk: pl.BlockSpec((None, r, D), lambda i: (layer, blk, 0))
    return pl.pallas_call(
        kern,
        grid=(T // TM,),
        in_specs=[rowspec(DC), rowspec(DML), rowspec(D), wspec(DC, 0), wspec(DML, DC // DML),
                  full(1, D), full(1, D), full(D, 2 * LANES), full(1, LANES), full(1, LANES)],
        out_specs=[rowspec(D), pl.BlockSpec((TM * nch, LANES), lambda i: (i, 0)), rowspec(LANES),
                   full(1, LANES)],
        out_shape=[
            jax.ShapeDtypeStruct((T, D), F32),
            jax.ShapeDtypeStruct((T * nch, LANES), jnp.uint32),
            jax.ShapeDtypeStruct((T, LANES), F32),
            jax.ShapeDtypeStruct((1, LANES), F32),
        ],
        scratch_shapes=[pltpu.VMEM((1, LANES), F32)],
        compiler_params=_params("arbitrary"),
        name="outproj_router",
    )(c, h, x, w_out, w_out, g, b, rw, rb, cnt0)


def _scatter_kernel(dest_ref, pend_ref, x_ref, *rest, TMS, RB, NE, first):
    if first:
        xb_out, zeros, sem = rest

        @pl.when(pl.program_id(0) == 0)
        def _():
            zeros[...] = jnp.zeros(zeros.shape, zeros.dtype)

            def pad_copy(e):
                start = pl.multiple_of(pend_ref[e] - RB, RB)
                return pltpu.make_async_copy(zeros, xb_out.at[pl.ds(start, RB)], sem)

            def has_rows(e):
                return pend_ref[e] > (pend_ref[e - 1] if e > 0 else 0)

            for e in range(NE):
                @pl.when(has_rows(e))
                def _():
                    pad_copy(e).start()

            for e in range(NE):
                @pl.when(has_rows(e))
                def _():
                    pad_copy(e).wait()
    else:
        _, xb_out, sem = rest

    nch = xb_out.shape[1]

    def copy(r, kk):
        return pltpu.make_async_copy(x_ref.at[pl.ds(pl.multiple_of(r * nch, nch), nch)],
                                     xb_out.at[dest_ref[TOP_K * r + kk]], sem)

    def issue(r, carry):
        for kk in range(TOP_K):
            copy(r, kk).start(priority=kk % 2)
        return carry

    lax.fori_loop(0, TMS, issue, 0, unroll=DMA_UNROLL)

    def drain(r, carry):
        for kk in range(TOP_K):
            copy(r, kk).wait()
        return carry

    lax.fori_loop(0, TMS, drain, 0, unroll=DMA_UNROLL)


def _scatter_rows(dest_flat, pend, xp, xb, P, RB):
    T = dest_flat.shape[0] // TOP_K
    nch = xp.shape[0] // T
    NE = pend.shape[0]
    TMS = _tile(T, 512)
    first = xb is None
    kern = functools.partial(_scatter_kernel, TMS=TMS, RB=RB, NE=NE, first=first)
    in_specs = [
        pl.BlockSpec((TOP_K * TMS,), lambda i: (i,), memory_space=pltpu.SMEM),
        pl.BlockSpec((NE,), lambda i: (0,), memory_space=pltpu.SMEM),
        pl.BlockSpec((TMS * nch, LANES), lambda i: (i, 0)),
    ]
    scratch = [pltpu.SemaphoreType.DMA(())]
    args = [dest_flat, pend, xp]
    if first:
        scratch = [pltpu.VMEM((RB, nch, LANES), xp.dtype)] + scratch
    else:
        in_specs.append(pl.BlockSpec(memory_space=pl.ANY))
        args.append(xb)
    return pl.pallas_call(
        kern,
        grid=(T // TMS,),
        in_specs=in_specs,
        out_specs=pl.BlockSpec(memory_space=pl.ANY),
        out_shape=jax.ShapeDtypeStruct((P, nch, LANES), xp.dtype),
        scratch_shapes=scratch,
        input_output_aliases={} if first else {3: 0},
        compiler_params=_params("arbitrary"),
        name="moe_scatter",
    )(*args)


def _new_expert(i, blk_e):
    return (i == 0) | (blk_e[i] != blk_e[jnp.maximum(i - 1, 0)])


def _pack_halves(x):
    h = x.shape[-1] // 2
    lo = lax.bitcast_convert_type(x[:, :h].astype(BF16).astype(F32), jnp.uint32)
    hi = lax.bitcast_convert_type(x[:, h:].astype(BF16).astype(F32), jnp.uint32)
    return (hi & jnp.uint32(0xFFFF0000)) | (lo >> 16)


def _unpack_halves(w):
    lo = lax.bitcast_convert_type(w << 16, F32)
    hi = lax.bitcast_convert_type(w & jnp.uint32(0xFFFF0000), F32)
    return lo, hi


def _expert_kernel(blk_e, blk_x, nxt_e, n_used, xb_ref, wg_hbm, wu_hbm, wd_hbm, y_ref,
                   stg, stu, std, wgs, wus, wds, sems, *, layer):
    del blk_x
    i = pl.program_id(0)
    half = wgs.shape[0] // 2
    nch = half // LANES
    RB = xb_ref.shape[0] // nch

    def weight_copies(e):
        return (pltpu.make_async_copy(wg_hbm.at[layer, e], stg, sems.at[0]),
                pltpu.make_async_copy(wu_hbm.at[layer, e], stu, sems.at[1]),
                pltpu.make_async_copy(wd_hbm.at[layer, e], std, sems.at[2]))

    @pl.when(i < n_used[0])
    def _():
        @pl.when(_new_expert(i, blk_e))
        def _():
            @pl.when(i == 0)
            def _():
                for cp in weight_copies(blk_e[i]):
                    cp.start()

            for cp in weight_copies(blk_e[i]):
                cp.wait()
            wgs[...] = stg[...].astype(BF16)
            wus[...] = stu[...].astype(BF16)
            wds[...] = std[...].astype(BF16)

            @pl.when(nxt_e[i] >= 0)
            def _():
                for cp in weight_copies(nxt_e[i]):
                    cp.start()

        lo, hi = _unpack_halves(_load_tiled_rows(xb_ref, RB, nch))
        lo = lo.astype(BF16)
        hi = hi.astype(BF16)
        g = (jnp.dot(lo, wgs[0:half, :], preferred_element_type=F32)
             + jnp.dot(hi, wgs[half:2 * half, :], preferred_element_type=F32))
        u = (jnp.dot(lo, wus[0:half, :], preferred_element_type=F32)
             + jnp.dot(hi, wus[half:2 * half, :], preferred_element_type=F32))
        hid = (g * jax.nn.sigmoid(g) * u).astype(BF16)
        _store_tiled_rows(y_ref, _pack_halves(jnp.dot(hid, wds[...], preferred_element_type=F32)), nch)


def _expert_ffn(layer, blk_e, blk_x, nxt_e, n_used, xb, w_gate, w_up, w_down, RB):
    D, DFF = w_gate.shape[-2:]
    half = D // 2
    nch = half // LANES
    P = xb.shape[0] // nch
    rows = pl.BlockSpec((RB * nch, LANES), lambda i, be, bx, nx, nu: (bx[i], 0))
    anyspec = pl.BlockSpec(memory_space=pl.ANY)
    return pl.pallas_call(
        functools.partial(_expert_kernel, layer=layer),
        grid_spec=pltpu.PrefetchScalarGridSpec(
            num_scalar_prefetch=4,
            grid=(P // RB,),
            in_specs=[rows, anyspec, anyspec, anyspec],
            out_specs=rows,
            scratch_shapes=[pltpu.VMEM((D, DFF), F32), pltpu.VMEM((D, DFF), F32), pltpu.VMEM((DFF, D), F32),
                            pltpu.VMEM((D, DFF), BF16), pltpu.VMEM((D, DFF), BF16), pltpu.VMEM((DFF, D), BF16),
                            pltpu.SemaphoreType.DMA((3,))],
        ),
        out_shape=jax.ShapeDtypeStruct((P * nch, LANES), jnp.uint32),
        compiler_params=_params("arbitrary"),
        name="expert_ffn",
    )(blk_e, blk_x, nxt_e, n_used, xb, w_gate, w_up, w_down)


def _combine_kernel(dest_ref, dnext_ref, x1_ref, info_ref, g_ref, b_ref, yb_hbm, x2_ref, x2b_ref,
                    bufs, sems, *, alpha, TMC):
    i = pl.program_id(0)
    nt = pl.num_programs(0)
    nch = yb_hbm.shape[1]

    def copy(dref, slot, r, kk):
        return pltpu.make_async_copy(yb_hbm.at[dref[TOP_K * r + kk]],
                                     bufs.at[slot, kk, pl.ds(pl.multiple_of(r * nch, nch), nch)],
                                     sems.at[slot])

    def issue(dref, slot):
        def body(r, carry):
            for kk in range(TOP_K):
                copy(dref, slot, r, kk).start(priority=kk % 2)
            return carry

        lax.fori_loop(0, TMC, body, 0, unroll=DMA_UNROLL)

    def drain(slot):
        def body(r, carry):
            for kk in range(TOP_K):
                copy(dest_ref, slot, r, kk).wait()
            return carry

        lax.fori_loop(0, TMC, body, 0, unroll=DMA_UNROLL)

    @pl.when(i == 0)
    def _():
        issue(dest_ref, 0)

    for slot in range(2):
        @pl.when((i % 2 == slot) & (i + 1 < nt))
        def _():
            issue(dnext_ref, 1 - slot)

    for slot in range(2):
        @pl.when(i % 2 == slot)
        def _():
            drain(slot)
            _combine_rows(bufs.at[slot], x1_ref, info_ref, g_ref, b_ref, x2_ref, x2b_ref,
                          alpha=alpha, TMC=TMC, nch=nch)


def _combine_rows(buf, x1_ref, info_ref, g_ref, b_ref, x2_ref, x2b_ref, *, alpha, TMC, nch):
    info = info_ref[...]
    g0 = info[:, 4:5]
    g1 = info[:, 5:6]
    half = nch * LANES

    def rows_of(kk):
        w = jnp.concatenate([buf[kk, pl.ds(c, TMC, stride=nch), :] for c in range(nch)], axis=1)
        return _unpack_halves(w)

    lo0, hi0 = rows_of(0)
    lo1, hi1 = rows_of(1)
    zl = alpha * x1_ref[:, 0:half] + (g0 * lo0 + g1 * lo1)
    zh = alpha * x1_ref[:, half:2 * half] + (g0 * hi0 + g1 * hi1)
    inv_d = 1.0 / (2 * half)
    mu = (jnp.sum(zl, axis=-1, keepdims=True) + jnp.sum(zh, axis=-1, keepdims=True)) * inv_d
    dl = zl - mu
    dh_ = zh - mu
    var = (jnp.sum(dl * dl, axis=-1, keepdims=True) + jnp.sum(dh_ * dh_, axis=-1, keepdims=True)) * inv_d
    rs = lax.rsqrt(var + LN_EPS)
    xl = dl * rs * g_ref[:, 0:half] + b_ref[:, 0:half]
    xh = dh_ * rs * g_ref[:, half:2 * half] + b_ref[:, half:2 * half]
    x2_ref[:, 0:half] = xl
    x2_ref[:, half:2 * half] = xh
    x2b_ref[:, 0:half] = xl.astype(BF16)
    x2b_ref[:, half:2 * half] = xh.astype(BF16)


def _combine_ln(dest_flat, x1, info, g, b, yb, alpha):
    T, D = x1.shape
    TMC = _tile(T, 512)
    kern = functools.partial(_combine_kernel, alpha=alpha, TMC=TMC)
    rowspec = lambda w: pl.BlockSpec((TMC, w), lambda i: (i, 0))
    NT = T // TMC
    return pl.pallas_call(
        kern,
        grid=(NT,),
        in_specs=[
            pl.BlockSpec((TOP_K * TMC,), lambda i: (i,), memory_space=pltpu.SMEM),
            pl.BlockSpec((TOP_K * TMC,), lambda i: (jnp.minimum(i + 1, NT - 1),), memory_space=pltpu.SMEM),
            rowspec(D), rowspec(LANES),
            pl.BlockSpec((1, D), lambda i: (0, 0)), pl.BlockSpec((1, D), lambda i: (0, 0)),
            pl.BlockSpec(memory_space=pl.ANY),
        ],
        out_specs=[rowspec(D), rowspec(D)],
        out_shape=[jax.ShapeDtypeStruct((T, D), F32), jax.ShapeDtypeStruct((T, D), BF16)],
        scratch_shapes=[pltpu.VMEM((2, TOP_K, TMC * (D // 2 // LANES), LANES), jnp.uint32),
                        pltpu.SemaphoreType.DMA((2,))],
        compiler_params=_params("arbitrary"),
        name="moe_combine",
    )(dest_flat, dest_flat, x1, info, g, b, yb)


def _dispatch_plan(cnt, infos, NE, RB, NBLK):
    counts = cnt[0, :NE].astype(jnp.int32)
    pcounts = (counts + RB - 1) // RB * RB
    pend = jnp.cumsum(pcounts)
    pstart = pend - pcounts
    dests = []
    for info in infos:
        e = info[:, 0:TOP_K].astype(jnp.int32)
        rank = info[:, TOP_K:2 * TOP_K].astype(jnp.int32)
        onehot = (e[..., None] == jnp.arange(NE, dtype=jnp.int32)).astype(jnp.int32)
        dests.append((jnp.sum(onehot * pstart, axis=-1) + rank).reshape(-1))
    n_used = jnp.maximum(pend[-1] // RB, 1).astype(jnp.int32)
    blk_x = jnp.minimum(jnp.arange(NBLK, dtype=jnp.int32), n_used - 1)
    blk_e = jnp.sum((pend[None, :] <= (blk_x * RB)[:, None]).astype(jnp.int32), axis=1)
    blk_e = jnp.minimum(blk_e, NE - 1)
    ids = jnp.arange(NE, dtype=jnp.int32)
    later = (ids[None, :] > ids[:, None]) & (counts > 0)[None, :]
    nxt = jnp.min(jnp.where(later, ids[None, :], NE), axis=1)
    nxt = jnp.where(nxt == NE, -1, nxt)
    nxt_e = jnp.sum((blk_e[:, None] == ids[None, :]).astype(jnp.int32) * nxt[None, :], axis=1)
    return dests, pend.astype(jnp.int32), blk_e, blk_x, nxt_e.astype(jnp.int32), n_used.reshape(1)


def kernel(x_prompt, x_sample, state_conv, state_C, state_n, state_m, w_in, b_in, conv_w, conv_b,
           conv_ln_g, conv_ln_b, ml_norm_g, w_out, ln1_g, ln1_b, router_w, router_bias,
           w_gate, w_up, w_down, ln2_g, ln2_b):
    depth, D, _ = w_in.shape
    DC = conv_w.shape[-1]
    NH, dh = state_C.shape[2], state_C.shape[3]
    DML = NH * dh
    NE = router_w.shape[1]
    alpha = (2.0 * depth) ** 0.25
    KW = conv_w.shape[1]

    streams = []
    Bp, Lp, _ = x_prompt.shape
    Bs, Ls, _ = x_sample.shape
    streams.append(dict(B=Bp, L=Lp, x=x_prompt.reshape(Bp * Lp, D), prompt=True))
    streams.append(dict(B=Bs, L=Ls, x=x_sample.reshape(Bs * Ls, D), prompt=False))
    for s in streams:
        s['xb'] = s['x']
        s['outs'] = dict(conv=[], C=[], n=[], m=[])

    A = TOP_K * (Bp * Lp + Bs * Ls)
    RB = ROW_BLOCK if A >= 32 * ROW_BLOCK else 32
    NBLK = -(-A // RB) + NE
    P = NBLK * RB

    rw_f = jnp.zeros((D, LANES), F32).at[:, :NE].set(router_w.astype(F32))
    rw_hi = rw_f.astype(BF16)
    rw = jnp.concatenate([rw_hi, (rw_f - rw_hi.astype(F32)).astype(BF16)], axis=1)
    rb = jnp.zeros((1, LANES), F32).at[0, :NE].set(router_bias.astype(F32))
    row2 = lambda v: v.reshape(1, -1)

    w_in_b = w_in.astype(BF16)
    w_out_b = w_out.astype(BF16)
    b_in3 = b_in.reshape(depth, 1, -1)
    gcol = 2 * DC + 4 * DML

    for l in range(depth):
        wgt = jnp.zeros((D, LANES), BF16).at[:, :2 * NH].set(w_in[l, :, gcol:].astype(BF16))
        bgt = jnp.zeros((1, LANES), F32).at[0, :2 * NH].set(b_in[l, gcol:])

        cnt = jnp.zeros((1, LANES), F32)
        for s in streams:
            B, L = s['B'], s['L']
            u, gates, *xb_new = _glu_proj(s['xb'], w_in_b, b_in3, l, DC, wgt, bgt)
            if xb_new:
                s['xb'] = xb_new[0]
            qkvo = _mm_bias(s['xb'], w_in_b, b_in3, l, 2 * DC, 4 * DML, BF16)
            if s['prompt']:
                hist = jnp.zeros((B, KW - 1, DC), F32)
                c0 = jnp.zeros((B, NH, dh, dh), F32)
                n0 = jnp.zeros((B, NH, 1, dh), F32)
                m0 = jnp.zeros((B, NH, 1, LANES), F32)
            else:
                hist = state_conv[l]
                c0 = state_C[l]
                n0 = state_n[l].reshape(B, NH, 1, dh)
                m0 = jnp.broadcast_to(state_m[l].reshape(B, NH, 1, 1), (B, NH, 1, LANES))
            c, new_hist = _conv_branch(u.reshape(B, L, DC), hist, conv_w[l], row2(conv_b[l]),
                                       row2(conv_ln_g[l]), row2(conv_ln_b[l]))
            gates_t = gates[:, :2 * NH].reshape(B, L, 2 * NH).transpose(0, 2, 1)
            hout, cN, nN, mN = _mlstm_branch(qkvo, gates, gates_t, c0, n0, m0, row2(ml_norm_g[l]), B, L)
            s['outs']['conv'].append(new_hist)
            s['outs']['C'].append(cN)
            s['outs']['n'].append(nN.reshape(B, NH, dh))
            s['outs']['m'].append(mN[:, :, 0, 0])
            x1, xp, info, cnt = _outproj_router(c.reshape(B * L, DC), hout, s['x'], w_out_b, l,
                                                row2(ln1_g[l]), row2(ln1_b[l]), rw, rb, cnt, alpha, NE)
            s['x1'], s['xp'], s['info'] = x1, xp, info

        dests, pend, blk_e, blk_x, nxt_e, n_used = _dispatch_plan(
            cnt, [s['info'] for s in streams], NE, RB, NBLK)
        xb = None
        for s, dest in zip(streams, dests):
            xb = _scatter_rows(dest, pend, s['xp'], xb, P, RB)
        nch = D // 2 // LANES
        yb = _expert_ffn(l, blk_e, blk_x, nxt_e, n_used, xb.reshape(P * nch, LANES), w_gate, w_up, w_down, RB)
        yb = yb.reshape(P, nch, LANES)
        for s, dest in zip(streams, dests):
            s['x'], s['xb'] = _combine_ln(dest, s['x1'], s['info'], row2(ln2_g[l]), row2(ln2_b[l]), yb, alpha)

    outs = []
    for s in streams:
        outs.append(s['x'].reshape(s['B'], s['L'], D))
    for s in streams:
        o = s['outs']
        outs += [jnp.stack(o['conv']), jnp.stack(o['C']), jnp.stack(o['n']), jnp.stack(o['m'])]
    return tuple(outs)
```

```python
import functools

import jax
import jax.numpy as jnp
from jax import lax
from jax.experimental import pallas as pl
from jax.experimental.pallas import tpu as pltpu

F32 = jnp.float32
BF16 = jnp.bfloat16

N_GROUPS = 4
TOP_K = 2
LN_EPS = 1e-5
LANES = 128
SUBLANES = 8
VMEM_LIMIT = 56 * 1024 * 1024
DMA_UNROLL = 8
ROW_BLOCK = 256
ROUTE_SPLIT = 1


def _tile(n, target, align=SUBLANES):
    t = min(n, target)
    t -= t % align
    while t >= align:
        if n % t == 0:
            return t
        t -= align
    return n


def _params(*sem):
    return pltpu.CompilerParams(dimension_semantics=sem, vmem_limit_bytes=VMEM_LIMIT)


def _ln_rows(y, g, b):
    mu = jnp.mean(y, axis=-1, keepdims=True)
    d = y - mu
    var = jnp.mean(d * d, axis=-1, keepdims=True)
    return d * lax.rsqrt(var + LN_EPS) * g + b


def _log_sigmoid(x):
    return jnp.minimum(x, 0.0) - jnp.log1p(jnp.exp(-jnp.abs(x)))


def _load_tiled_rows(ref, n, nch, row0=0):
    return jnp.concatenate([ref[pl.ds(row0 * nch + c, n, stride=nch), :] for c in range(nch)], axis=1)


def _store_tiled_rows(ref, val, nch, row0=0):
    n = val.shape[0]
    for c in range(nch):
        ref[pl.ds(row0 * nch + c, n, stride=nch), :] = val[:, c * LANES:(c + 1) * LANES]


def _glu_kernel(x_ref, wa_ref, wg_ref, ba_ref, bg_ref, wgt_ref, bgt_ref, u_ref, gates_ref, *xb_ref):
    if xb_ref:
        @pl.when(pl.program_id(1) == 0)
        def _():
            xb_ref[0][...] = x_ref[...].astype(BF16)

        x = xb_ref[0][...]
    else:
        x = x_ref[...]
    a = jnp.dot(x, wa_ref[...], preferred_element_type=F32) + ba_ref[...]
    g = jnp.dot(x, wg_ref[...], preferred_element_type=F32) + bg_ref[...]
    u_ref[...] = a * jax.nn.sigmoid(g)

    @pl.when(pl.program_id(1) == 0)
    def _():
        gates_ref[...] = jnp.dot(x, wgt_ref[...], preferred_element_type=F32) + bgt_ref[...]


def _glu_proj(xb, w_in, b_in, layer, DC, wgt, bgt):
    T, D = xb.shape
    TM = _tile(T, 1024)
    TN = _tile(DC, 512, LANES)
    goff = DC // TN
    out_specs = [
        pl.BlockSpec((TM, TN), lambda m, n: (m, n)),
        pl.BlockSpec((TM, LANES), lambda m, n: (m, 0)),
    ]
    out_shape = [jax.ShapeDtypeStruct((T, DC), F32), jax.ShapeDtypeStruct((T, LANES), F32)]
    if xb.dtype != BF16:
        out_specs.append(pl.BlockSpec((TM, D), lambda m, n: (m, 0)))
        out_shape.append(jax.ShapeDtypeStruct((T, D), BF16))
    return pl.pallas_call(
        _glu_kernel,
        grid=(T // TM, DC // TN),
        in_specs=[
            pl.BlockSpec((TM, D), lambda m, n: (m, 0)),
            pl.BlockSpec((None, D, TN), lambda m, n: (layer, 0, n)),
            pl.BlockSpec((None, D, TN), lambda m, n: (layer, 0, n + goff)),
            pl.BlockSpec((None, 1, TN), lambda m, n: (layer, 0, n)),
            pl.BlockSpec((None, 1, TN), lambda m, n: (layer, 0, n + goff)),
            pl.BlockSpec((D, LANES), lambda m, n: (0, 0)),
            pl.BlockSpec((1, LANES), lambda m, n: (0, 0)),
        ],
        out_specs=out_specs,
        out_shape=out_shape,
        compiler_params=_params("parallel", "arbitrary"),
        name="glu_proj",
    )(xb, w_in, w_in, b_in, b_in, wgt, bgt)


def _mm_kernel(x_ref, w_ref, b_ref, o_ref):
    acc = jnp.dot(x_ref[...], w_ref[...], preferred_element_type=F32) + b_ref[...]
    o_ref[...] = acc.astype(o_ref.dtype)


def _mm_bias(xb, w_in, b_in, layer, col0, N, out_dtype):
    T, D = xb.shape
    TM = _tile(T, 1024)
    TN = next(t for t in range(min(N, 1024), 0, -LANES) if N % t == 0 and col0 % t == 0)
    noff = col0 // TN
    return pl.pallas_call(
        _mm_kernel,
        grid=(T // TM, N // TN),
        in_specs=[
            pl.BlockSpec((TM, D), lambda m, n: (m, 0)),
            pl.BlockSpec((None, D, TN), lambda m, n: (layer, 0, n + noff)),
            pl.BlockSpec((None, 1, TN), lambda m, n: (layer, 0, n + noff)),
        ],
        out_specs=pl.BlockSpec((TM, TN), lambda m, n: (m, n)),
        out_shape=jax.ShapeDtypeStruct((T, N), out_dtype),
        compiler_params=_params("parallel", "parallel"),
        name="qkvo_proj",
    )(xb, w_in, b_in)


def _conv_kernel(u_ref, hist_ref, w_ref, cb_ref, g_ref, b_ref, c_ref, nh_ref, ext, cbuf,
                 *, TT, KW, HP, RC, CC):
    t = pl.program_id(1)
    H = KW - 1
    DC = u_ref.shape[-1]

    @pl.when(t == 0)
    def _():
        ext[HP - H:HP, :] = hist_ref[...]

    ext[HP:HP + TT, :] = u_ref[...]

    WIN = RC + HP
    for r0 in range(0, TT, RC):
        for c0 in range(0, DC, CC):
            win = ext[r0:r0 + WIN, c0:c0 + CC]
            acc = None
            for sh in range(SUBLANES):
                shifted = win if sh == 0 else pltpu.roll(win, WIN - sh, axis=0)
                for s in range(sh, HP + 1, SUBLANES):
                    j = s - (HP - H)
                    if 0 <= j < KW:
                        term = w_ref[j:j + 1, c0:c0 + CC] * shifted[s - sh:s - sh + RC]
                        acc = term if acc is None else acc + term
            cbuf[r0:r0 + RC, c0:c0 + CC] = acc + cb_ref[:, c0:c0 + CC]

    y = _ln_rows(cbuf[...], g_ref[...], b_ref[...])
    c_ref[...] = (y * jax.nn.sigmoid(y)).astype(c_ref.dtype)

    tail = ext[HP + TT - H:HP + TT, :]
    ext[HP - H:HP, :] = tail

    @pl.when(t == pl.num_programs(1) - 1)
    def _():
        nh_ref[...] = tail


def _conv_branch(u, hist, layer, conv_w, conv_b, g, b):
    B, L, DC = u.shape
    KW = conv_w.shape[0]
    TT = _tile(L, 512)
    assert TT >= KW - 1 and TT % SUBLANES == 0
    HP = -(-(KW - 1) // SUBLANES) * SUBLANES
    RC = _tile(TT, 128)
    CC = LANES
    kern = functools.partial(_conv_kernel, TT=TT, KW=KW, HP=HP, RC=RC, CC=CC)
    return pl.pallas_call(
        kern,
        grid=(B, L // TT),
        in_specs=[
            pl.BlockSpec((None, TT, DC), lambda bi, t: (bi, t, 0)),
            pl.BlockSpec((None, None, KW - 1, DC), lambda bi, t: (layer, bi, 0, 0)),
            pl.BlockSpec((KW, DC), lambda bi, t: (0, 0)),
            pl.BlockSpec((1, DC), lambda bi, t: (0, 0)),
            pl.BlockSpec((1, DC), lambda bi, t: (0, 0)),
            pl.BlockSpec((1, DC), lambda bi, t: (0, 0)),
        ],
        out_specs=[
            pl.BlockSpec((None, TT, DC), lambda bi, t: (bi, t, 0)),
            pl.BlockSpec((None, KW - 1, DC), lambda bi, t: (bi, 0, 0)),
        ],
        out_shape=[jax.ShapeDtypeStruct((B, L, DC), BF16), jax.ShapeDtypeStruct((B, KW - 1, DC), F32)],
        scratch_shapes=[pltpu.VMEM((HP + TT, DC), F32), pltpu.VMEM((TT, DC), F32)],
        compiler_params=_params("parallel", "arbitrary"),
        name="conv_branch",
    )(u, hist, conv_w, conv_b, g, b)


def _mlstm_kernel(q_ref, k_ref, v_ref, o_ref, gc_ref, gr_ref, c0_ref, n0_ref, m0_ref, mg_ref,
                  h_ref, cN_ref, nN_ref, mN_ref, c_scr, n_scr, m_scr, *, NH, LC):
    ck = pl.program_id(1)
    dh = c_scr.shape[-1]
    scale = dh ** -0.5

    @pl.when(ck == 0)
    def _():
        c_scr[...] = c0_ref[...]
        n_scr[...] = n0_ref[...]
        m_scr[...] = m0_ref[...]

    gcol = gc_ref[...]
    lane = lax.broadcasted_iota(jnp.int32, gcol.shape, 1)
    grow = gr_ref[...]
    ti = lax.broadcasted_iota(jnp.int32, (LC, LC), 0)
    si = lax.broadcasted_iota(jnp.int32, (LC, LC), 1)
    causal = si <= ti
    anti = ti <= si

    for hd in range(NH):
        cols = slice(hd * dh, (hd + 1) * dh)
        li_col = jnp.sum(jnp.where(lane == hd, gcol, 0.0), axis=1, keepdims=True)
        lf_col = _log_sigmoid(jnp.sum(jnp.where(lane == NH + hd, gcol, 0.0), axis=1, keepdims=True))
        li_row = grow[hd:hd + 1, :]
        lf_row = _log_sigmoid(grow[NH + hd:NH + hd + 1, :])

        f_col = jnp.sum(jnp.where(causal, lf_row, 0.0), axis=1, keepdims=True)
        f_row = jnp.sum(jnp.where(anti, lf_col, 0.0), axis=0, keepdims=True)
        g_row = li_row - f_row
        g_col = li_col - f_col

        m_prev = m_scr[hd, :, 0:1]
        a = jnp.where(causal, g_row, -jnp.inf)
        m_col = jnp.maximum(m_prev, jnp.max(a, axis=1, keepdims=True))
        dmat = jnp.exp(a - m_col)
        inter = jnp.exp(m_prev - m_col)

        q = q_ref[:, cols]
        k = k_ref[:, cols]
        v = v_ref[:, cols]
        c_prev = c_scr[hd]
        n_prev = n_scr[hd]
        s = lax.dot_general(q, k, (((1,), (1,)), ((), ())), preferred_element_type=F32) * scale * dmat
        num = (jnp.dot(s.astype(BF16), v, preferred_element_type=F32)
               + inter * jnp.dot(q, c_prev.astype(BF16), preferred_element_type=F32))
        den = (jnp.sum(s, axis=1, keepdims=True)
               + inter * jnp.sum(q.astype(F32) * n_prev, axis=1, keepdims=True))
        hcore = num / jnp.maximum(jnp.abs(den), jnp.exp(-(f_col + m_col)))

        mu = jnp.mean(hcore, axis=1, keepdims=True)
        d = hcore - mu
        var = jnp.mean(d * d, axis=1, keepdims=True)
        hn = d * lax.rsqrt(var + LN_EPS)
        gate = jax.nn.sigmoid(o_ref[:, cols].astype(F32))
        h_ref[:, cols] = (gate * hn * mg_ref[:, cols]).astype(h_ref.dtype)

        m_last = m_col[LC - 1:LC, :]
        f_tot = f_col[LC - 1:LC, :]
        decay = jnp.exp(m_prev - m_last)
        kw = k.astype(F32) * (jnp.exp(g_col - m_last) * scale)
        c_new = decay * c_prev + lax.dot_general(kw.astype(BF16), v, (((0,), (0,)), ((), ())),
                                                 preferred_element_type=F32)
        n_new = decay * n_prev + jnp.sum(kw, axis=0, keepdims=True)
        m_new = jnp.broadcast_to(f_tot + m_last, (1, LANES))
        c_scr[hd] = c_new
        n_scr[hd] = n_new
        m_scr[hd] = m_new
        cN_ref[hd] = c_new
        nN_ref[hd] = n_new
        mN_ref[hd] = m_new


def _mlstm_branch(qkvo, gates, gates_t, c0, n0, m0, layer, mln_g, B, L):
    NH, dh = c0.shape[2], c0.shape[3]
    DML = NH * dh
    LC = _tile(L, 256, LANES) if L % LANES == 0 else L
    NC = L // LC
    kern = functools.partial(_mlstm_kernel, NH=NH, LC=LC)
    qkv_spec = lambda off: pl.BlockSpec((LC, DML), lambda b, c: (b * NC + c, off))
    state4 = lambda r, w: pl.BlockSpec((None, NH, r, w), lambda b, c: (b, 0, 0, 0))
    state_in = lambda r, w: pl.BlockSpec((None, None, NH, r, w), lambda b, c: (layer, b, 0, 0, 0))
    return pl.pallas_call(
        kern,
        grid=(B, NC),
        in_specs=[
            qkv_spec(0), qkv_spec(1), qkv_spec(2), qkv_spec(3),
            pl.BlockSpec((LC, LANES), lambda b, c: (b * NC + c, 0)),
            pl.BlockSpec((None, 2 * NH, LC), lambda b, c: (b, 0, c)),
            state_in(dh, dh), state_in(1, dh), state_in(1, LANES),
            pl.BlockSpec((1, DML), lambda b, c: (0, 0)),
        ],
        out_specs=[
            pl.BlockSpec((LC, DML), lambda b, c: (b * NC + c, 0)),
            state4(dh, dh), state4(1, dh), state4(1, LANES),
        ],
        out_shape=[
            jax.ShapeDtypeStruct((B * L, DML), BF16),
            jax.ShapeDtypeStruct((B, NH, dh, dh), F32),
            jax.ShapeDtypeStruct((B, NH, 1, dh), F32),
            jax.ShapeDtypeStruct((B, NH, 1, LANES), F32),
        ],
        scratch_shapes=[pltpu.VMEM((NH, dh, dh), F32), pltpu.VMEM((NH, 1, dh), F32),
                        pltpu.VMEM((NH, 1, LANES), F32)],
        compiler_params=_params("parallel", "arbitrary"),
        name="mlstm_branch",
    )(qkvo, qkvo, qkvo, qkvo, gates, gates_t, c0, n0, m0, mln_g)


def _outproj_kernel(c_ref, h_ref, x_ref, wc_ref, wh_ref, g_ref, b_ref, rw_ref, rb_ref, cnt0_ref,
                    x1_ref, xp_ref, info_ref, cnt_ref, run_scr, *, alpha, NE):
    i = pl.program_id(0)
    TM, D = x_ref.shape
    half = D // 2
    epg = NE // N_GROUPS
    neg = -jnp.inf

    @pl.when(i == 0)
    def _():
        run_scr[...] = cnt0_ref[...]

    SM = TM // ROUTE_SPLIT if TM % (ROUTE_SPLIT * 2 * SUBLANES) == 0 else TM
    lane = lax.broadcasted_iota(jnp.int32, (SM, LANES), 1)
    lanef = lane.astype(F32)
    ri = lax.broadcasted_iota(jnp.int32, (SM, SM), 0)
    ci = lax.broadcasted_iota(jnp.int32, (SM, SM), 1)
    tri = jnp.where(ci < ri, 1.0, 0.0).astype(BF16)
    run = run_scr[...]

    for r0_ in range(0, TM, SM):
        rs = slice(r0_, r0_ + SM)
        mix = (jnp.dot(c_ref[rs, :], wc_ref[...], preferred_element_type=F32)
               + jnp.dot(h_ref[rs, :], wh_ref[...], preferred_element_type=F32))
        x1 = _ln_rows(alpha * x_ref[rs, :] + mix, g_ref[...], b_ref[...])
        x1_ref[rs, :] = x1

        x1h = x1.astype(BF16)
        x1hf = x1h.astype(F32)
        lo = lax.bitcast_convert_type(x1hf[:, :half], jnp.uint32)
        hi = lax.bitcast_convert_type(x1hf[:, half:], jnp.uint32)
        _store_tiled_rows(xp_ref, (hi & jnp.uint32(0xFFFF0000)) | (lo >> 16), half // LANES, r0_)

        x1l = (x1 - x1hf).astype(BF16)
        p = jnp.dot(x1h, rw_ref[...], preferred_element_type=F32)
        logits = (p[:, :LANES] + p[:, LANES:]
                  + jnp.dot(x1l, rw_ref[:, :LANES], preferred_element_type=F32))
        aff = jax.nn.sigmoid(logits)
        biased = jnp.where(lane < NE, aff + rb_ref[...], neg)
        best = e0 = e1 = None
        for grp in range(N_GROUPS):
            vg = jnp.where((lane >= grp * epg) & (lane < (grp + 1) * epg), biased, neg)
            m1 = jnp.max(vg, axis=1, keepdims=True)
            i1 = jnp.min(jnp.where(vg == m1, lanef, float(LANES)), axis=1, keepdims=True)
            vg2 = jnp.where(lanef == i1, neg, vg)
            m2 = jnp.max(vg2, axis=1, keepdims=True)
            i2 = jnp.min(jnp.where(vg2 == m2, lanef, float(LANES)), axis=1, keepdims=True)
            score = m1 + m2
            if grp == 0:
                best, e0, e1 = score, i1, i2
            else:
                better = score > best
                best = jnp.where(better, score, best)
                e0 = jnp.where(better, i1, e0)
                e1 = jnp.where(better, i2, e1)

        oh0 = lanef == e0
        oh1 = lanef == e1
        a0 = jnp.sum(jnp.where(oh0, aff, 0.0), axis=1, keepdims=True)
        a1 = jnp.sum(jnp.where(oh1, aff, 0.0), axis=1, keepdims=True)
        asum = a0 + a1
        onehot = jnp.where(oh0 | oh1, 1.0, 0.0)
        prefix = jnp.dot(tri, onehot.astype(BF16), preferred_element_type=F32) + run
        rk0 = jnp.sum(jnp.where(oh0, prefix, 0.0), axis=1, keepdims=True)
        rk1 = jnp.sum(jnp.where(oh1, prefix, 0.0), axis=1, keepdims=True)
        run = run + jnp.sum(onehot, axis=0, keepdims=True)

        cols = (e0, e1, rk0, rk1, a0 / asum, a1 / asum)
        info = jnp.zeros((SM, LANES), F32)
        for ci_, val in enumerate(cols):
            info = jnp.where(lane == ci_, val, info)
        info_ref[rs, :] = info

    run_scr[...] = run
    cnt_ref[...] = run


def _outproj_router(c, h, x, w_out, layer, g, b, rw, rb, cnt0, alpha, NE):
    T, D = x.shape
    DC, DML = c.shape[1], h.shape[1]
    assert DC % DML == 0
    TM = _tile(T, 512)
    nch = D // 2 // LANES
    kern = functools.partial(_outproj_kernel, alpha=alpha, NE=NE)
    rowspec = lambda w: pl.BlockSpec((TM, w), lambda i: (i, 0))
    full = lambda r, w: pl.BlockSpec((r, w), lambda i: (0, 0))
    wspec = lambda r, blk: pl.BlockSpec((None, r, D), lambda i: (layer, blk, 0))
    return pl.pallas_call(
        kern,
        grid=(T // TM,),
        in_specs=[rowspec(DC), rowspec(DML), rowspec(D), wspec(DC, 0), wspec(DML, DC // DML),
                  full(1, D), full(1, D), full(D, 2 * LANES), full(1, LANES), full(1, LANES)],
        out_specs=[rowspec(D), pl.BlockSpec((TM * nch, LANES), lambda i: (i, 0)), rowspec(LANES),
                   full(1, LANES)],
        out_shape=[
            jax.ShapeDtypeStruct((T, D), F32),
            jax.ShapeDtypeStruct((T * nch, LANES), jnp.uint32),
            jax.ShapeDtypeStruct((T, LANES), F32),
            jax.ShapeDtypeStruct((1, LANES), F32),
        ],
        scratch_shapes=[pltpu.VMEM((1, LANES), F32)],
        compiler_params=_params("arbitrary"),
        name="outproj_router",
    )(c, h, x, w_out, w_out, g, b, rw, rb, cnt0)


def _scatter_kernel(dest_ref, pend_ref, x_ref, *rest, TMS, RB, NE, first):
    if first:
        xb_out, zeros, sem = rest

        @pl.when(pl.program_id(0) == 0)
        def _():
            zeros[...] = jnp.zeros(zeros.shape, zeros.dtype)

            def pad_copy(e):
                start = pl.multiple_of(pend_ref[e] - RB, RB)
                return pltpu.make_async_copy(zeros, xb_out.at[pl.ds(start, RB)], sem)

            def has_rows(e):
                return pend_ref[e] > (pend_ref[e - 1] if e > 0 else 0)

            for e in range(NE):
                @pl.when(has_rows(e))
                def _():
                    pad_copy(e).start()

            for e in range(NE):
                @pl.when(has_rows(e))
                def _():
                    pad_copy(e).wait()
    else:
        _, xb_out, sem = rest

    nch = xb_out.shape[1]

    def copy(r, kk):
        return pltpu.make_async_copy(x_ref.at[pl.ds(pl.multiple_of(r * nch, nch), nch)],
                                     xb_out.at[dest_ref[TOP_K * r + kk]], sem)

    def issue(r, carry):
        for kk in range(TOP_K):
            copy(r, kk).start(priority=kk % 2)
        return carry

    lax.fori_loop(0, TMS, issue, 0, unroll=DMA_UNROLL)

    def drain(r, carry):
        for kk in range(TOP_K):
            copy(r, kk).wait()
        return carry

    lax.fori_loop(0, TMS, drain, 0, unroll=DMA_UNROLL)


def _scatter_rows(dest_flat, pend, xp, xb, P, RB):
    T = dest_flat.shape[0] // TOP_K
    nch = xp.shape[0] // T
    NE = pend.shape[0]
    TMS = _tile(T, 512)
    first = xb is None
    kern = functools.partial(_scatter_kernel, TMS=TMS, RB=RB, NE=NE, first=first)
    in_specs = [
        pl.BlockSpec((TOP_K * TMS,), lambda i: (i,), memory_space=pltpu.SMEM),
        pl.BlockSpec((NE,), lambda i: (0,), memory_space=pltpu.SMEM),
        pl.BlockSpec((TMS * nch, LANES), lambda i: (i, 0)),
    ]
    scratch = [pltpu.SemaphoreType.DMA(())]
    args = [dest_flat, pend, xp]
    if first:
        scratch = [pltpu.VMEM((RB, nch, LANES), xp.dtype)] + scratch
    else:
        in_specs.append(pl.BlockSpec(memory_space=pl.ANY))
        args.append(xb)
    return pl.pallas_call(
        kern,
        grid=(T // TMS,),
        in_specs=in_specs,
        out_specs=pl.BlockSpec(memory_space=pl.ANY),
        out_shape=jax.ShapeDtypeStruct((P, nch, LANES), xp.dtype),
        scratch_shapes=scratch,
        input_output_aliases={} if first else {3: 0},
        compiler_params=_params("arbitrary"),
        name="moe_scatter",
    )(*args)


def _new_expert(i, blk_e):
    return (i == 0) | (blk_e[i] != blk_e[jnp.maximum(i - 1, 0)])


def _pack_halves(x):
    h = x.shape[-1] // 2
    lo = lax.bitcast_convert_type(x[:, :h].astype(BF16).astype(F32), jnp.uint32)
    hi = lax.bitcast_convert_type(x[:, h:].astype(BF16).astype(F32), jnp.uint32)
    return (hi & jnp.uint32(0xFFFF0000)) | (lo >> 16)


def _unpack_halves(w):
    lo = lax.bitcast_convert_type(w << 16, F32)
    hi = lax.bitcast_convert_type(w & jnp.uint32(0xFFFF0000), F32)
    return lo, hi


def _expert_kernel(blk_e, blk_x, nxt_e, n_used, xb_ref, wg_hbm, wu_hbm, wd_hbm, y_ref,
                   stg, stu, std, wgs, wus, wds, sems, *, layer):
    del blk_x
    i = pl.program_id(0)
    half = wgs.shape[0] // 2
    nch = half // LANES
    RB = xb_ref.shape[0] // nch

    def weight_copies(e):
        return (pltpu.make_async_copy(wg_hbm.at[layer, e], stg, sems.at[0]),
                pltpu.make_async_copy(wu_hbm.at[layer, e], stu, sems.at[1]),
                pltpu.make_async_copy(wd_hbm.at[layer, e], std, sems.at[2]))

    @pl.when(i < n_used[0])
    def _():
        @pl.when(_new_expert(i, blk_e))
        def _():
            @pl.when(i == 0)
            def _():
                for cp in weight_copies(blk_e[i]):
                    cp.start()

            for cp in weight_copies(blk_e[i]):
                cp.wait()
            wgs[...] = stg[...].astype(BF16)
            wus[...] = stu[...].astype(BF16)
            wds[...] = std[...].astype(BF16)

            @pl.when(nxt_e[i] >= 0)
            def _():
                for cp in weight_copies(nxt_e[i]):
                    cp.start()

        lo, hi = _unpack_halves(_load_tiled_rows(xb_ref, RB, nch))
        lo = lo.astype(BF16)
        hi = hi.astype(BF16)
        g = (jnp.dot(lo, wgs[0:half, :], preferred_element_type=F32)
             + jnp.dot(hi, wgs[half:2 * half, :], preferred_element_type=F32))
        u = (jnp.dot(lo, wus[0:half, :], preferred_element_type=F32)
             + jnp.dot(hi, wus[half:2 * half, :], preferred_element_type=F32))
        hid = (g * jax.nn.sigmoid(g) * u).astype(BF16)
        _store_tiled_rows(y_ref, _pack_halves(jnp.dot(hid, wds[...], preferred_element_type=F32)), nch)


def _expert_ffn(layer, blk_e, blk_x, nxt_e, n_used, xb, w_gate, w_up, w_down, RB):
    D, DFF = w_gate.shape[-2:]
    half = D // 2
    nch = half // LANES
    P = xb.shape[0] // nch
    rows = pl.BlockSpec((RB * nch, LANES), lambda i, be, bx, nx, nu: (bx[i], 0))
    anyspec = pl.BlockSpec(memory_space=pl.ANY)
    return pl.pallas_call(
        functools.partial(_expert_kernel, layer=layer),
        grid_spec=pltpu.PrefetchScalarGridSpec(
            num_scalar_prefetch=4,
            grid=(P // RB,),
            in_specs=[rows, anyspec, anyspec, anyspec],
            out_specs=rows,
            scratch_shapes=[pltpu.VMEM((D, DFF), F32), pltpu.VMEM((D, DFF), F32), pltpu.VMEM((DFF, D), F32),
                            pltpu.VMEM((D, DFF), BF16), pltpu.VMEM((D, DFF), BF16), pltpu.VMEM((DFF, D), BF16),
                            pltpu.SemaphoreType.DMA((3,))],
        ),
        out_shape=jax.ShapeDtypeStruct((P * nch, LANES), jnp.uint32),
        compiler_params=_params("arbitrary"),
        name="expert_ffn",
    )(blk_e, blk_x, nxt_e, n_used, xb, w_gate, w_up, w_down)


def _combine_kernel(dest_ref, dnext_ref, x1_ref, info_ref, g_ref, b_ref, yb_hbm, x2_ref, x2b_ref,
                    bufs, sems, *, alpha, TMC):
    i = pl.program_id(0)
    nt = pl.num_programs(0)
    nch = yb_hbm.shape[1]

    def copy(dref, slot, r, kk):
        return pltpu.make_async_copy(yb_hbm.at[dref[TOP_K * r + kk]],
                                     bufs.at[slot, kk, pl.ds(pl.multiple_of(r * nch, nch), nch)],
                                     sems.at[slot])

    def issue(dref, slot):
        def body(r, carry):
            for kk in range(TOP_K):
                copy(dref, slot, r, kk).start(priority=kk % 2)
            return carry

        lax.fori_loop(0, TMC, body, 0, unroll=DMA_UNROLL)

    def drain(slot):
        def body(r, carry):
            for kk in range(TOP_K):
                copy(dest_ref, slot, r, kk).wait()
            return carry

        lax.fori_loop(0, TMC, body, 0, unroll=DMA_UNROLL)

    @pl.when(i == 0)
    def _():
        issue(dest_ref, 0)

    for slot in range(2):
        @pl.when((i % 2 == slot) & (i + 1 < nt))
        def _():
            issue(dnext_ref, 1 - slot)

    for slot in range(2):
        @pl.when(i % 2 == slot)
        def _():
            drain(slot)
            _combine_rows(bufs.at[slot], x1_ref, info_ref, g_ref, b_ref, x2_ref, x2b_ref,
                          alpha=alpha, TMC=TMC, nch=nch)


def _combine_rows(buf, x1_ref, info_ref, g_ref, b_ref, x2_ref, x2b_ref, *, alpha, TMC, nch):
    info = info_ref[...]
    g0 = info[:, 4:5]
    g1 = info[:, 5:6]
    half = nch * LANES

    def rows_of(kk):
        w = jnp.concatenate([buf[kk, pl.ds(c, TMC, stride=nch), :] for c in range(nch)], axis=1)
        return _unpack_halves(w)

    lo0, hi0 = rows_of(0)
    lo1, hi1 = rows_of(1)
    zl = alpha * x1_ref[:, 0:half] + (g0 * lo0 + g1 * lo1)
    zh = alpha * x1_ref[:, half:2 * half] + (g0 * hi0 + g1 * hi1)
    inv_d = 1.0 / (2 * half)
    mu = (jnp.sum(zl, axis=-1, keepdims=True) + jnp.sum(zh, axis=-1, keepdims=True)) * inv_d
    dl = zl - mu
    dh_ = zh - mu
    var = (jnp.sum(dl * dl, axis=-1, keepdims=True) + jnp.sum(dh_ * dh_, axis=-1, keepdims=True)) * inv_d
    rs = lax.rsqrt(var + LN_EPS)
    xl = dl * rs * g_ref[:, 0:half] + b_ref[:, 0:half]
    xh = dh_ * rs * g_ref[:, half:2 * half] + b_ref[:, half:2 * half]
    x2_ref[:, 0:half] = xl
    x2_ref[:, half:2 * half] = xh
    x2b_ref[:, 0:half] = xl.astype(BF16)
    x2b_ref[:, half:2 * half] = xh.astype(BF16)


def _combine_ln(dest_flat, x1, info, g, b, yb, alpha):
    T, D = x1.shape
    TMC = _tile(T, 512)
    kern = functools.partial(_combine_kernel, alpha=alpha, TMC=TMC)
    rowspec = lambda w: pl.BlockSpec((TMC, w), lambda i: (i, 0))
    NT = T // TMC
    return pl.pallas_call(
        kern,
        grid=(NT,),
        in_specs=[
            pl.BlockSpec((TOP_K * TMC,), lambda i: (i,), memory_space=pltpu.SMEM),
            pl.BlockSpec((TOP_K * TMC,), lambda i: (jnp.minimum(i + 1, NT - 1),), memory_space=pltpu.SMEM),
            rowspec(D), rowspec(LANES),
            pl.BlockSpec((1, D), lambda i: (0, 0)), pl.BlockSpec((1, D), lambda i: (0, 0)),
            pl.BlockSpec(memory_space=pl.ANY),
        ],
        out_specs=[rowspec(D), rowspec(D)],
        out_shape=[jax.ShapeDtypeStruct((T, D), F32), jax.ShapeDtypeStruct((T, D), BF16)],
        scratch_shapes=[pltpu.VMEM((2, TOP_K, TMC * (D // 2 // LANES), LANES), jnp.uint32),
                        pltpu.SemaphoreType.DMA((2,))],
        compiler_params=_params("arbitrary"),
        name="moe_combine",
    )(dest_flat, dest_flat, x1, info, g, b, yb)


def _dispatch_plan(cnt, infos, NE, RB, NBLK):
    counts = cnt[0, :NE].astype(jnp.int32)
    pcounts = (counts + RB - 1) // RB * RB
    pend = jnp.cumsum(pcounts)
    pstart = pend - pcounts
    dests = []
    for info in infos:
        e = info[:, 0:TOP_K].astype(jnp.int32)
        rank = info[:, TOP_K:2 * TOP_K].astype(jnp.int32)
        onehot = (e[..., None] == jnp.arange(NE, dtype=jnp.int32)).astype(jnp.int32)
        dests.append((jnp.sum(onehot * pstart, axis=-1) + rank).reshape(-1))
    n_used = jnp.maximum(pend[-1] // RB, 1).astype(jnp.int32)
    blk_x = jnp.minimum(jnp.arange(NBLK, dtype=jnp.int32), n_used - 1)
    blk_e = jnp.sum((pend[None, :] <= (blk_x * RB)[:, None]).astype(jnp.int32), axis=1)
    blk_e = jnp.minimum(blk_e, NE - 1)
    ids = jnp.arange(NE, dtype=jnp.int32)
    later = (ids[None, :] > ids[:, None]) & (counts > 0)[None, :]
    nxt = jnp.min(jnp.where(later, ids[None, :], NE), axis=1)
    nxt = jnp.where(nxt == NE, -1, nxt)
    nxt_e = jnp.sum((blk_e[:, None] == ids[None, :]).astype(jnp.int32) * nxt[None, :], axis=1)
    return dests, pend.astype(jnp.int32), blk_e, blk_x, nxt_e.astype(jnp.int32), n_used.reshape(1)


def kernel(x_prompt, x_sample, state_conv, state_C, state_n, state_m, w_in, b_in, conv_w, conv_b,
           conv_ln_g, conv_ln_b, ml_norm_g, w_out, ln1_g, ln1_b, router_w, router_bias,
           w_gate, w_up, w_down, ln2_g, ln2_b):
    depth, D, _ = w_in.shape
    DC = conv_w.shape[-1]
    NH, dh = state_C.shape[2], state_C.shape[3]
    DML = NH * dh
    NE = router_w.shape[1]
    alpha = (2.0 * depth) ** 0.25
    KW = conv_w.shape[1]

    streams = []
    Bp, Lp, _ = x_prompt.shape
    Bs, Ls, _ = x_sample.shape
    streams.append(dict(B=Bp, L=Lp, x=x_prompt.reshape(Bp * Lp, D), prompt=True))
    streams.append(dict(B=Bs, L=Ls, x=x_sample.reshape(Bs * Ls, D), prompt=False))
    for s in streams:
        s['xb'] = s['x']
        s['outs'] = dict(conv=[], C=[], n=[], m=[])

    A = TOP_K * (Bp * Lp + Bs * Ls)
    RB = ROW_BLOCK if A >= 32 * ROW_BLOCK else 32
    NBLK = -(-A // RB) + NE
    P = NBLK * RB

    rw_f = jnp.zeros((D, LANES), F32).at[:, :NE].set(router_w.astype(F32))
    rw_hi = rw_f.astype(BF16)
    rw = jnp.concatenate([rw_hi, (rw_f - rw_hi.astype(F32)).astype(BF16)], axis=1)
    rb = jnp.zeros((1, LANES), F32).at[0, :NE].set(router_bias.astype(F32))
    row2 = lambda v: v.reshape(1, -1)

    w_in_b = w_in.astype(BF16)
    w_out_b = w_out.astype(BF16)
    b_in3 = b_in.reshape(depth, 1, -1)
    gcol = 2 * DC + 4 * DML

    for l in range(depth):
        wgt = jnp.zeros((D, LANES), BF16).at[:, :2 * NH].set(w_in[l, :, gcol:].astype(BF16))
        bgt = jnp.zeros((1, LANES), F32).at[0, :2 * NH].set(b_in[l, gcol:])

        cnt = jnp.zeros((1, LANES), F32)
        for s in streams:
            B, L = s['B'], s['L']
            u, gates, *xb_new = _glu_proj(s['xb'], w_in_b, b_in3, l, DC, wgt, bgt)
            if xb_new:
                s['xb'] = xb_new[0]
            qkvo = _mm_bias(s['xb'], w_in_b, b_in3, l, 2 * DC, 4 * DML, BF16)
            if s['prompt']:
                hist = jnp.zeros((1, B, KW - 1, DC), F32)
                c0 = jnp.zeros((1, B, NH, dh, dh), F32)
                n0 = jnp.zeros((1, B, NH, 1, dh), F32)
                m0 = jnp.zeros((1, B, NH, 1, LANES), F32)
                sl = 0
            else:
                hist = state_conv
                c0 = state_C
                n0 = state_n.reshape(depth, B, NH, 1, dh)
                m0 = jnp.broadcast_to(state_m.reshape(depth, B, NH, 1, 1), (depth, B, NH, 1, LANES))
                sl = l
            c, new_hist = _conv_branch(u.reshape(B, L, DC), hist, sl, conv_w[l], row2(conv_b[l]),
                                       row2(conv_ln_g[l]), row2(conv_ln_b[l]))
            gates_t = gates[:, :2 * NH].reshape(B, L, 2 * NH).transpose(0, 2, 1)
            hout, cN, nN, mN = _mlstm_branch(qkvo, gates, gates_t, c0, n0, m0, sl, row2(ml_norm_g[l]), B, L)
            s['outs']['conv'].append(new_hist)
            s['outs']['C'].append(cN)
            s['outs']['n'].append(nN.reshape(B, NH, dh))
            s['outs']['m'].append(mN[:, :, 0, 0])
            x1, xp, info, cnt = _outproj_router(c.reshape(B * L, DC), hout, s['x'], w_out_b, l,
                                                row2(ln1_g[l]), row2(ln1_b[l]), rw, rb, cnt, alpha, NE)
            s['x1'], s['xp'], s['info'] = x1, xp, info

        dests, pend, blk_e, blk_x, nxt_e, n_used = _dispatch_plan(
            cnt, [s['info'] for s in streams], NE, RB, NBLK)
        xb = None
        for s, dest in zip(streams, dests):
            xb = _scatter_rows(dest, pend, s['xp'], xb, P, RB)
        nch = D // 2 // LANES
        yb = _expert_ffn(l, blk_e, blk_x, nxt_e, n_used, xb.reshape(P * nch, LANES), w_gate, w_up, w_down, RB)
        yb = yb.reshape(P, nch, LANES)
        for s, dest in zip(streams, dests):
            s['x'], s['xb'] = _combine_ln(dest, s['x1'], s['info'], row2(ln2_g[l]), row2(ln2_b[l]), yb, alpha)

    outs = []
    for s in streams:
        outs.append(s['x'].reshape(s['B'], s['L'], D))
    for s in streams:
        o = s['outs']
        outs += [jnp.stack(o['conv']), jnp.stack(o['C']), jnp.stack(o['n']), jnp.stack(o['m'])]
    return tuple(outs)
```
